```python
import math
import jax
import jax.numpy as jnp
from jax import lax
import numpy as np

D_MODEL = 1024
BATCH = 8
SEQ = 2048
DEPTH = 2
DEC_BATCH = 128
DEC_SEQ = 8
PAST_LEN = 16384
PAGE_SIZE = 128

N_EVEN = (DEPTH + 1) // 2
N_ODD = DEPTH // 2

A_HEADS = 8
A_HEAD_DIM = 64
A_WIDTH = A_HEADS * A_HEAD_DIM
A_DECAY_LORA = 64
A_ICLR_LORA = 64
A_GATE_LORA = 128
A_COLS = 3 * A_WIDTH + A_DECAY_LORA + A_ICLR_LORA + A_GATE_LORA
A_GN_EPS = A_HEAD_DIM * 1e-5

B_HEADS = 4
B_QK_DIM = 64
B_V_DIM = 128
B_QK_WIDTH = B_HEADS * B_QK_DIM
B_V_WIDTH = B_HEADS * B_V_DIM
B_COLS = 2 * B_QK_WIDTH + 2 * B_V_WIDTH
RET_CHUNK = 64
ROPE_BASE = 10000.0

D_RNN = 1280
C_BLOCKS = 10
C_BLOCK_DIM = D_RNN // C_BLOCKS
CONV_W = 4
LRU_C = 8.0

D_FF = 2816
N_EXPERTS = 8
TOP_K = 2
D_FF_EXPERT = 3584
MOE_BLOCK = 256
NORM_EPS = 1e-6

kernel_name = 'rwkv7_retnet_rglru_moe_hybrid_step'


def _rms_norm(x, g):
    xf = x.astype(jnp.float32)
    y = xf * lax.rsqrt(jnp.mean(xf * xf, -1, keepdims=True) + NORM_EPS)
    return (y * g.astype(jnp.float32)).astype(x.dtype)


def _split_last(x, sizes):
    offs, acc = [], 0
    for s in sizes[:-1]:
        acc += s
        offs.append(acc)
    return jnp.split(x, offs, axis=-1)


def _rotary(t, pos):
    half = t.shape[-1] // 2
    inv = ROPE_BASE ** (-jnp.arange(half, dtype=jnp.float32) / half)
    ang = pos.astype(jnp.float32)[:, None] * inv[None, :]
    cos = jnp.cos(ang)[None, :, None, :]
    sin = jnp.sin(ang)[None, :, None, :]
    t1, t2 = t[..., :half], t[..., half:]
    return jnp.concatenate([t1 * cos - t2 * sin, t1 * sin + t2 * cos], -1)


def _swiglu(h, wg, wu, wd):
    return (jax.nn.silu(h @ wg) * (h @ wu)) @ wd


def _rwkv7_time_mix(p, shift_prev, wkv_prev, mu, w0, w_decay, a0, w_iclr, w_gate,
                    k_k, k_a, r_k, lnx_g, lnx_b):
    f32 = jnp.float32
    Bn, L, _ = p.shape
    prev = jnp.concatenate([shift_prev[:, None, :].astype(p.dtype), p[:, :-1]], axis=1)
    pm = p + (prev - p) * mu
    r, k, v, xw, xa, xg = _split_last(pm, [A_WIDTH, A_WIDTH, A_WIDTH, A_DECAY_LORA, A_ICLR_LORA, A_GATE_LORA])
    w_log = -jax.nn.softplus(-(w0 + jnp.tanh(xw) @ w_decay).astype(f32)) - 0.5
    decay = jnp.exp(-jnp.exp(w_log))
    a = jax.nn.sigmoid((a0 + xa @ w_iclr).astype(f32))
    g = (jax.nn.sigmoid(xg) @ w_gate).astype(f32)

    def heads(t):
        return t.astype(f32).reshape(Bn, L, A_HEADS, A_HEAD_DIM)

    def per_head(t):
        return t.astype(f32).reshape(A_HEADS, A_HEAD_DIM)

    r_h, k_h, v_h, w_h, a_h = heads(r), heads(k), heads(v), heads(decay), heads(a)
    kk = k_h * per_head(k_k)
    kk = kk / jnp.maximum(jnp.sqrt(jnp.sum(kk * kk, -1, keepdims=True)), 1e-12)
    k_h = k_h * (1.0 + (a_h - 1.0) * per_head(k_a))

    def step(S, inp):
        r_t, w_t, k_t, v_t, kk_t, a_t = inp
        sa = jnp.einsum('bhvk,bhk->bhv', S, -kk_t)
        S = (S * w_t[:, :, None, :] + sa[..., None] * (kk_t * a_t)[:, :, None, :]
             + v_t[..., None] * k_t[:, :, None, :])
        return S, jnp.einsum('bhvk,bhk->bhv', S, r_t)

    xs = tuple(jnp.moveaxis(t, 1, 0) for t in (r_h, w_h, k_h, v_h, kk, a_h))
    S_last, y = lax.scan(step, wkv_prev.astype(f32), xs)
    y = jnp.moveaxis(y, 0, 1)
    mean = jnp.mean(y, -1, keepdims=True)
    var = jnp.mean(jnp.square(y - mean), -1, keepdims=True)
    yn = (y - mean) * lax.rsqrt(var + A_GN_EPS)
    yn = yn * per_head(lnx_g) + per_head(lnx_b)
    bonus = jnp.sum(r_h * k_h * r_k.astype(f32), -1, keepdims=True) * v_h
    out = (yn + bonus).reshape(Bn, L, A_WIDTH) * g
    return out.astype(p.dtype), p[:, -1], S_last


def _retention_chunkwise(q, k, v, s0):
    Bn, L, H, _ = q.shape
    dv = v.shape[-1]
    C = math.gcd(L, RET_CHUNK)
    n = L // C
    log_g = jnp.log1p(-jnp.exp2(-5.0 - jnp.arange(H, dtype=jnp.float32)))
    idx = jnp.arange(C, dtype=jnp.float32)
    diff = idx[:, None] - idx[None, :]
    intra = jnp.where(diff >= 0, jnp.exp(jnp.maximum(diff, 0.0) * log_g[:, None, None]), 0.0)
    q_dec = jnp.exp((idx + 1.0)[:, None] * log_g[None, :])
    k_dec = jnp.exp((C - 1.0 - idx)[:, None] * log_g[None, :])
    c_dec = jnp.exp(C * log_g)

    def to_chunks(t):
        return jnp.moveaxis(t.reshape(Bn, n, C, H, t.shape[-1]), 1, 0)

    def step(s, inp):
        qc, kc, vc = inp
        scores = jnp.einsum('bihd,bjhd->bhij', qc, kc) * intra
        o = (jnp.einsum('bhij,bjhe->bihe', scores, vc)
             + jnp.einsum('bihd,bhde->bihe', qc * q_dec[:, :, None], s))
        s = s * c_dec[:, None, None] + jnp.einsum('bjhd,bjhe->bhde', kc * k_dec[:, :, None], vc)
        return s, o

    s_last, o = lax.scan(step, s0, (to_chunks(q), to_chunks(k), to_chunks(v)))
    return jnp.moveaxis(o, 0, 1).reshape(Bn, L, H, dv), s_last


def _retention_group(pb, pos, s_prev):
    f32 = jnp.float32
    Bn, L, _ = pb.shape
    q, k, v, g = _split_last(pb, [B_QK_WIDTH, B_QK_WIDTH, B_V_WIDTH, B_V_WIDTH])
    q = _rotary(q.reshape(Bn, L, B_HEADS, B_QK_DIM).astype(f32), pos) * (B_QK_DIM ** -0.5)
    k = _rotary(k.reshape(Bn, L, B_HEADS, B_QK_DIM).astype(f32), pos)
    v = v.reshape(Bn, L, B_HEADS, B_V_DIM).astype(f32)
    o, s_new = _retention_chunkwise(q, k, v, s_prev.astype(f32))
    o = o * lax.rsqrt(jnp.mean(o * o, -1, keepdims=True) + NORM_EPS)
    o = o.reshape(Bn, L, B_V_WIDTH) * jax.nn.silu(g.astype(f32))
    return o.astype(pb.dtype), s_new


def _rglru_block(h, conv_prev, h_prev, w_in, conv_w, conv_b, w_a, b_a, w_x, b_x, lam, w_out):
    f32 = jnp.float32
    Bn, L, _ = h.shape
    proj = h @ w_in
    gate, xb = proj[..., :D_RNN], proj[..., D_RNN:]
    full = jnp.concatenate([conv_prev.astype(xb.dtype), xb], axis=1)
    xc = conv_b + sum(full[:, j:j + L] * conv_w[j] for j in range(CONV_W))
    new_conv = full[:, L:]
    xcb = xc.reshape(Bn, L, C_BLOCKS, C_BLOCK_DIM)
    r = jax.nn.sigmoid((jnp.einsum('blnd,nde->blne', xcb, w_a).reshape(Bn, L, D_RNN) + b_a).astype(f32))
    i = jax.nn.sigmoid((jnp.einsum('blnd,nde->blne', xcb, w_x).reshape(Bn, L, D_RNN) + b_x).astype(f32))
    log_a = -LRU_C * r * jax.nn.softplus(-lam.astype(f32))
    a = jnp.exp(log_a)
    b = jnp.sqrt(-jnp.expm1(2.0 * log_a)) * i * xc.astype(f32)

    def step(hc, inp):
        a_t, b_t = inp
        hc = a_t * hc + b_t
        return hc, hc

    h_last, hs = lax.scan(step, h_prev.astype(f32), (jnp.moveaxis(a, 1, 0), jnp.moveaxis(b, 1, 0)))
    y = jax.nn.gelu(gate.astype(f32)) * jnp.moveaxis(hs, 0, 1)
    return y.astype(h.dtype) @ w_out, new_conv, h_last


def _moe_swiglu(x, w_router, w_gate, w_up, w_down):
    Bn, L, D = x.shape
    T = Bn * L
    x2d = x.reshape(T, D)
    logits = x2d.astype(jnp.float32) @ w_router.astype(jnp.float32)
    top_v, top_i = lax.top_k(logits, TOP_K)
    gates = jax.nn.softmax(top_v, axis=-1)
    flat_e = top_i.reshape(-1)
    flat_tok = jnp.repeat(jnp.arange(T, dtype=jnp.int32), TOP_K)
    flat_gate = gates.reshape(-1)
    order = jnp.argsort(flat_e)
    se, stok, sg = flat_e[order], flat_tok[order], flat_gate[order]
    counts = jnp.bincount(flat_e, length=N_EXPERTS)
    padded = ((counts + MOE_BLOCK - 1) // MOE_BLOCK) * MOE_BLOCK
    start = jnp.cumsum(counts) - counts
    pend = jnp.cumsum(padded)
    pstart = pend - padded
    dest = pstart[se] + (jnp.arange(T * TOP_K) - start[se])
    n_blocks = (T * TOP_K + MOE_BLOCK - 1) // MOE_BLOCK + N_EXPERTS
    P = n_blocks * MOE_BLOCK
    buf_tok = jnp.full((P,), T, jnp.int32).at[dest].set(stok)
    buf_gate = jnp.zeros((P,), jnp.float32).at[dest].set(sg)
    block_e = jnp.minimum(jnp.searchsorted(pend, jnp.arange(n_blocks) * MOE_BLOCK, side='right'),
                          N_EXPERTS - 1)
    x_pad = jnp.concatenate([x2d, jnp.zeros((1, D), x2d.dtype)], 0)

    def block_fn(args):
        tok, e = args
        xb = x_pad[tok]
        return _swiglu(xb, w_gate[e], w_up[e], w_down[e])

    yb = lax.map(block_fn, (buf_tok.reshape(n_blocks, MOE_BLOCK), block_e)).reshape(P, D)
    yb = yb * buf_gate[:, None].astype(yb.dtype)
    out = jnp.zeros((T + 1, D), yb.dtype).at[buf_tok].add(yb)[:T]
    return out.reshape(Bn, L, D).astype(x.dtype)


def _trunk(x, pos0, shift, wkv, ret, conv, hlru, w):
    Bn, L, _ = x.shape
    pos = pos0 + jnp.arange(L)
    n_shift, n_wkv, n_ret, n_conv, n_h = [], [], [], [], []
    for layer in range(DEPTH):
        i = layer // 2
        if layer % 2 == 0:
            h = _rms_norm(x, w['norm_mix0'][i])
            proj = h @ w['w_in0'][i]
            ya, s_shift, s_wkv = _rwkv7_time_mix(
                proj[..., :A_COLS], shift[i], wkv[i], w['a_mu'][i], w['a_w0'][i], w['a_w_decay'][i],
                w['a_a0'][i], w['a_w_iclr'][i], w['a_w_gate'][i], w['a_k_k'][i], w['a_k_a'][i],
                w['a_r_k'][i], w['a_lnx_g'][i], w['a_lnx_b'][i])
            yb, s_ret = _retention_group(proj[..., A_COLS:], pos, ret[i])
            x = x + jnp.concatenate([ya, yb], -1) @ w['w_out0'][i]
            x = x + _swiglu(_rms_norm(x, w['norm_ffn0'][i]), w['ffn_gate'][i], w['ffn_up'][i], w['ffn_down'][i])
            n_shift.append(s_shift)
            n_wkv.append(s_wkv)
            n_ret.append(s_ret)
        else:
            yc, s_conv, s_h = _rglru_block(
                _rms_norm(x, w['norm_mix1'][i]), conv[i], hlru[i], w['w_in1'][i], w['c_conv_w'][i],
                w['c_conv_b'][i], w['c_w_a'][i], w['c_b_a'][i], w['c_w_x'][i], w['c_b_x'][i],
                w['c_lambda'][i], w['w_out1'][i])
            x = x + yc
            x = x + _moe_swiglu(_rms_norm(x, w['norm_ffn1'][i]), w['moe_router'][i], w['moe_gate'][i],
                                w['moe_up'][i], w['moe_down'][i])
            n_conv.append(s_conv)
            n_h.append(s_h)
    dt = x.dtype
    states = (jnp.stack(n_shift).astype(dt), jnp.stack(n_wkv).astype(dt), jnp.stack(n_ret).astype(dt),
              jnp.stack(n_conv).astype(dt), jnp.stack(n_h).astype(dt))
    return _rms_norm(x, w['norm_final']), states


def setup_inputs(seed: int = 0) -> dict:
    key = jax.random.key(seed)
    ks = iter(jax.random.split(key, 48))
    f32 = jnp.float32

    def normal(shape, scale):
        return jax.random.normal(next(ks), shape, f32) * scale

    def gain(shape):
        return 1.0 + 0.05 * jax.random.normal(next(ks), shape, f32)

    def unif(shape, lo, hi):
        return jax.random.uniform(next(ks), shape, f32, lo, hi)

    E0, E1 = N_EVEN, N_ODD
    inp = {}
    inp['x_prompt'] = normal((BATCH, SEQ, D_MODEL), 1.0)
    inp['x_sample'] = normal((DEC_BATCH, DEC_SEQ, D_MODEL), 1.0)
    inp['state_rwkv_shift'] = normal((E0, DEC_BATCH, A_COLS), 1.0)
    inp['state_rwkv_wkv'] = normal((E0, DEC_BATCH, A_HEADS, A_HEAD_DIM, A_HEAD_DIM), 0.5)
    inp['state_ret'] = normal((E0, DEC_BATCH, B_HEADS, B_QK_DIM, B_V_DIM), 2.0)
    inp['state_lru_conv'] = normal((E1, DEC_BATCH, CONV_W - 1, D_RNN), 1.0)
    inp['state_lru_h'] = normal((E1, DEC_BATCH, D_RNN), 0.5)
    inp['norm_mix0'] = gain((E0, D_MODEL))
    inp['w_in0'] = normal((E0, D_MODEL, A_COLS + B_COLS), D_MODEL ** -0.5)
    inp['a_mu'] = unif((E0, A_COLS), 0.1, 0.9)
    inp['a_w0'] = unif((E0, A_WIDTH), -6.0, 1.0)
    inp['a_w_decay'] = normal((E0, A_DECAY_LORA, A_WIDTH), 0.5 * A_DECAY_LORA ** -0.5)
    inp['a_a0'] = normal((E0, A_WIDTH), 0.5)
    inp['a_w_iclr'] = normal((E0, A_ICLR_LORA, A_WIDTH), 0.5 * A_ICLR_LORA ** -0.5)
    inp['a_w_gate'] = normal((E0, A_GATE_LORA, A_WIDTH), A_GATE_LORA ** -0.5)
    inp['a_k_k'] = 0.85 + normal((E0, A_WIDTH), 0.05)
    inp['a_k_a'] = gain((E0, A_WIDTH))
    inp['a_r_k'] = normal((E0, A_HEADS, A_HEAD_DIM), 0.1)
    inp['a_lnx_g'] = gain((E0, A_WIDTH))
    inp['a_lnx_b'] = normal((E0, A_WIDTH), 0.01)
    inp['w_out0'] = normal((E0, A_WIDTH + B_V_WIDTH, D_MODEL), (A_WIDTH + B_V_WIDTH) ** -0.5)
    inp['norm_ffn0'] = gain((E0, D_MODEL))
    inp['ffn_gate'] = normal((E0, D_MODEL, D_FF), D_MODEL ** -0.5)
    inp['ffn_up'] = normal((E0, D_MODEL, D_FF), D_MODEL ** -0.5)
    inp['ffn_down'] = normal((E0, D_FF, D_MODEL), D_FF ** -0.5)
    inp['norm_mix1'] = gain((E1, D_MODEL))
    inp['w_in1'] = normal((E1, D_MODEL, 2 * D_RNN), D_MODEL ** -0.5)
    inp['c_conv_w'] = normal((E1, CONV_W, D_RNN), 0.5)
    inp['c_conv_b'] = normal((E1, D_RNN), 0.01)
    inp['c_w_a'] = normal((E1, C_BLOCKS, C_BLOCK_DIM, C_BLOCK_DIM), C_BLOCK_DIM ** -0.5)
    inp['c_b_a'] = normal((E1, D_RNN), 0.01)
    inp['c_w_x'] = normal((E1, C_BLOCKS, C_BLOCK_DIM, C_BLOCK_DIM), C_BLOCK_DIM ** -0.5)
    inp['c_b_x'] = normal((E1, D_RNN), 0.01)
    a_base = unif((E1, D_RNN), 0.9, 0.999)
    inp['c_lambda'] = jnp.log(a_base) - jnp.log1p(-a_base)
    inp['w_out1'] = normal((E1, D_RNN, D_MODEL), D_RNN ** -0.5)
    inp['norm_ffn1'] = gain((E1, D_MODEL))
    inp['moe_router'] = normal((E1, D_MODEL, N_EXPERTS), D_MODEL ** -0.5)
    inp['moe_gate'] = normal((E1, N_EXPERTS, D_MODEL, D_FF_EXPERT), D_MODEL ** -0.5)
    inp['moe_up'] = normal((E1, N_EXPERTS, D_MODEL, D_FF_EXPERT), D_MODEL ** -0.5)
    inp['moe_down'] = normal((E1, N_EXPERTS, D_FF_EXPERT, D_MODEL), D_FF_EXPERT ** -0.5)
    inp['norm_final'] = gain((D_MODEL,))
    return inp


def reference(x_prompt, x_sample, state_rwkv_shift, state_rwkv_wkv, state_ret, state_lru_conv, state_lru_h,
              norm_mix0, w_in0, a_mu, a_w0, a_w_decay, a_a0, a_w_iclr, a_w_gate, a_k_k, a_k_a, a_r_k,
              a_lnx_g, a_lnx_b, w_out0, norm_ffn0, ffn_gate, ffn_up, ffn_down, norm_mix1, w_in1,
              c_conv_w, c_conv_b, c_w_a, c_b_a, c_w_x, c_b_x, c_lambda, w_out1, norm_ffn1,
              moe_router, moe_gate, moe_up, moe_down, norm_final):
    w = dict(norm_mix0=norm_mix0, w_in0=w_in0, a_mu=a_mu, a_w0=a_w0, a_w_decay=a_w_decay, a_a0=a_a0,
             a_w_iclr=a_w_iclr, a_w_gate=a_w_gate, a_k_k=a_k_k, a_k_a=a_k_a, a_r_k=a_r_k,
             a_lnx_g=a_lnx_g, a_lnx_b=a_lnx_b, w_out0=w_out0, norm_ffn0=norm_ffn0, ffn_gate=ffn_gate,
             ffn_up=ffn_up, ffn_down=ffn_down, norm_mix1=norm_mix1, w_in1=w_in1, c_conv_w=c_conv_w,
             c_conv_b=c_conv_b, c_w_a=c_w_a, c_b_a=c_b_a, c_w_x=c_w_x, c_b_x=c_b_x, c_lambda=c_lambda,
             w_out1=w_out1, norm_ffn1=norm_ffn1, moe_router=moe_router, moe_gate=moe_gate,
             moe_up=moe_up, moe_down=moe_down, norm_final=norm_final)
    dt = x_prompt.dtype
    bp = x_prompt.shape[0]
    y_prompt, st_p = _trunk(
        x_prompt, 0,
        jnp.zeros((N_EVEN, bp, A_COLS), dt),
        jnp.zeros((N_EVEN, bp, A_HEADS, A_HEAD_DIM, A_HEAD_DIM), dt),
        jnp.zeros((N_EVEN, bp, B_HEADS, B_QK_DIM, B_V_DIM), dt),
        jnp.zeros((N_ODD, bp, CONV_W - 1, D_RNN), dt),
        jnp.zeros((N_ODD, bp, D_RNN), dt), w)
    y_sample, st_s = _trunk(x_sample, PAST_LEN, state_rwkv_shift, state_rwkv_wkv, state_ret,
                            state_lru_conv, state_lru_h, w)
    shift_p, wkv_p, ret_p, conv_p, h_p = st_p
    shift_s, wkv_s, ret_s, conv_s, h_s = st_s
    return (y_prompt, y_sample, shift_p, wkv_p, ret_p, conv_p, h_p, shift_s, wkv_s, ret_s, conv_s, h_s)
```

```python
import functools
import math

import jax
import jax.numpy as jnp
from jax import lax
from jax.experimental import pallas as pl
from jax.experimental.pallas import tpu as pltpu

F32 = jnp.float32
BF16 = jnp.bfloat16

A_HEADS = 8
A_HEAD_DIM = 64
A_WIDTH = A_HEADS * A_HEAD_DIM
A_LORA_COLS = 256
A_COLS = 3 * A_WIDTH + A_LORA_COLS
A_GN_EPS = A_HEAD_DIM * 1e-5
B_HEADS = 4
B_QK_DIM = 64
B_V_DIM = 128
B_QK_WIDTH = B_HEADS * B_QK_DIM
B_V_WIDTH = B_HEADS * B_V_DIM
B_COLS = 2 * B_QK_WIDTH + 2 * B_V_WIDTH
RET_CHUNK = 64
ROPE_BASE = 10000.0
D_RNN = 1280
C_BLOCKS = 10
C_BLOCK_DIM = D_RNN // C_BLOCKS
CONV_W = 4
LRU_C = 8.0
N_EXPERTS = 8
TOP_K = 2
NORM_EPS = 1e-6
PAST_LEN = 16384

LANES = 128
SUBLANES = 8
VMEM_BUDGET_BYTES = 56 * 1024 * 1024


def _tile(n, pref, mult=SUBLANES):
    t = min(pref, n)
    while t > mult and (n % t or t % mult):
        t -= 1
    assert n % t == 0 and t % mult == 0, (n, pref, mult)
    return t


def _params(sem, est_bytes):
    limit = int(min(max(est_bytes * 5 // 4 + (4 << 20), 32 << 20), VMEM_BUDGET_BYTES))
    return pltpu.CompilerParams(dimension_semantics=sem, vmem_limit_bytes=limit)


def _rms(x, g):
    return x * lax.rsqrt(jnp.mean(x * x, -1, keepdims=True) + NORM_EPS) * g


def _dot(a, b):
    return jnp.dot(a, b, preferred_element_type=F32)


def _sigmoid(x):
    return 1.0 / (1.0 + jnp.exp(-x))


def _seg_sum(x, ones):
    hi = x.astype(BF16)
    lo = (x - hi.astype(F32)).astype(BF16)
    return _dot(hi, ones) + _dot(lo, ones)


def _norm_matmul_kernel(x_ref, g_ref, w_ref, *o_refs, splits):
    h = _rms(x_ref[...], g_ref[...]).astype(BF16)
    off = 0
    for o_ref, n in zip(o_refs, splits):
        o_ref[...] = _dot(h, w_ref[:, off:off + n])
        off += n


def _norm_matmul(x, g, w, splits):
    t, d = x.shape
    n = w.shape[1]
    tm = _tile(t, 512)
    est = 2 * (tm * d * 4 + d * n * 2 + tm * n * 4) + tm * n * 4
    return pl.pallas_call(
        functools.partial(_norm_matmul_kernel, splits=splits),
        grid=(t // tm,),
        in_specs=[pl.BlockSpec((tm, d), lambda i: (i, 0)),
                  pl.BlockSpec((1, d), lambda i: (0, 0)),
                  pl.BlockSpec((d, n), lambda i: (0, 0))],
        out_specs=[pl.BlockSpec((tm, s), lambda i: (i, 0)) for s in splits],
        out_shape=[jax.ShapeDtypeStruct((t, s), F32) for s in splits],
        compiler_params=_params(("parallel",), est),
        name="norm_matmul",
    )(x, g.reshape(1, d), w)


def _rwkv_prep_kernel(p_ref, ext_ref, mu_ref, w0_ref, wdec_ref, a0_ref, wiclr_ref, wgate_ref,
                      kk_ref, ka_ref, ones_ref,
                      r_o, w_o, k_o, v_o, kk_o, b_o, g_o, *, seq_len, tm):
    p = p_ref[...]
    rolled = pltpu.roll(p, 1, 0)
    row = lax.broadcasted_iota(jnp.int32, (tm, 1), 0)
    if seq_len >= tm:
        prev = jnp.where(row == 0, ext_ref[0:1, :], rolled)
    else:
        prev = jnp.where(row % seq_len == 0, ext_ref[...], rolled)
    pm = p + (prev - p) * mu_ref[...]
    r = pm[:, 0:A_WIDTH]
    k = pm[:, A_WIDTH:2 * A_WIDTH]
    v = pm[:, 2 * A_WIDTH:3 * A_WIDTH]
    tail = pm[:, 3 * A_WIDTH:A_COLS]
    w_pre = w0_ref[...] + _dot(jnp.tanh(tail).astype(BF16), wdec_ref[...])
    w_log = jnp.minimum(w_pre, 0.0) - jnp.log1p(jnp.exp(-jnp.abs(w_pre))) - 0.5
    decay = jnp.exp(-jnp.exp(w_log))
    a = _sigmoid(a0_ref[...] + _dot(tail.astype(BF16), wiclr_ref[...]))
    g = _dot(_sigmoid(tail).astype(BF16), wgate_ref[...])
    kk = k * kk_ref[...]
    ss = _seg_sum(kk * kk, ones_ref[...])
    kkn = kk / jnp.maximum(jnp.sqrt(ss), 1e-12)
    r_o[...] = r
    w_o[...] = decay
    k_o[...] = k * (1.0 + (a - 1.0) * ka_ref[...])
    v_o[...] = v
    kk_o[...] = kkn
    b_o[...] = -(kkn * a)
    g_o[...] = g


def _rwkv_prep(pa, ext, seq_len, tm, wts):
    t = pa.shape[0]
    full = lambda shape: pl.BlockSpec(shape, lambda i: (0,) * len(shape))
    ext_rows = SUBLANES if seq_len >= tm else tm
    est = 2 * (tm * A_COLS * 4 * 2 + 7 * tm * A_WIDTH * 4) + 12 * tm * A_WIDTH * 4
    return pl.pallas_call(
        functools.partial(_rwkv_prep_kernel, seq_len=seq_len, tm=tm),
        grid=(t // tm,),
        in_specs=[pl.BlockSpec((tm, A_COLS), lambda i: (i, 0)),
                  pl.BlockSpec((ext_rows, A_COLS), lambda i: (i, 0)),
                  full((1, A_COLS)), full((1, A_WIDTH)), full((A_LORA_COLS, A_WIDTH)),
                  full((1, A_WIDTH)), full((A_LORA_COLS, A_WIDTH)), full((A_LORA_COLS, A_WIDTH)),
                  full((1, A_WIDTH)), full((1, A_WIDTH)), full((A_WIDTH, A_WIDTH))],
        out_specs=[pl.BlockSpec((tm, A_WIDTH), lambda i: (i, 0))] * 7,
        out_shape=[jax.ShapeDtypeStruct((t, A_WIDTH), F32)] * 7,
        compiler_params=_params(("parallel",), est),
        name="rwkv_prep",
    )(pa, ext, *wts)


def _rwkv_scan_kernel(r_ref, w_ref, k_ref, kk_ref, b_ref, v_ref, s0_ref, y_ref, sout_ref, s_scr,
                      *, tl, nv):
    tb = pl.program_id(1)
    nvg = nv // SUBLANES
    kdim = A_HEAD_DIM

    @pl.when(tb == 0)
    def _():
        s_scr[...] = s0_ref[...]

    def bcast(ref, t, k):
        return jnp.broadcast_to(ref[t, pl.ds(k, 1), :], (SUBLANES, LANES))

    def step(t, carry):
        sa = [jnp.zeros((SUBLANES, LANES), F32) for _ in range(nvg)]
        for k in range(kdim):
            kk_row = bcast(kk_ref, t, k)
            for vg in range(nvg):
                sa[vg] = sa[vg] + s_scr[k, vg * SUBLANES:(vg + 1) * SUBLANES, :] * kk_row
        vv = [v_ref[t, vg * SUBLANES:(vg + 1) * SUBLANES, :] for vg in range(nvg)]
        y = [jnp.zeros((SUBLANES, LANES), F32) for _ in range(nvg)]
        for k in range(kdim):
            w_row = bcast(w_ref, t, k)
            b_row = bcast(b_ref, t, k)
            k_row = bcast(k_ref, t, k)
            r_row = bcast(r_ref, t, k)
            for vg in range(nvg):
                rows = slice(vg * SUBLANES, (vg + 1) * SUBLANES)
                s_new = s_scr[k, rows, :] * w_row + sa[vg] * b_row + vv[vg] * k_row
                s_scr[k, rows, :] = s_new
                y[vg] = y[vg] + s_new * r_row
        for vg in range(nvg):
            y_ref[t, vg * SUBLANES:(vg + 1) * SUBLANES, :] = y[vg]
        return carry

    lax.fori_loop(0, tl, step, 0)

    @pl.when(tb == pl.num_programs(1) - 1)
    def _():
        sout_ref[...] = s_scr[...]


def _rwkv_scan(kvecs, vvec, s0, tl):
    length, kdim, nl = kvecs[0].shape
    nv = vvec.shape[1]
    kspec = pl.BlockSpec((tl, kdim, LANES), lambda g, tb: (tb, 0, g))
    vspec = pl.BlockSpec((tl, nv, LANES), lambda g, tb: (tb, 0, g))
    sspec = pl.BlockSpec((kdim, nv, LANES), lambda g, tb: (0, 0, g))
    est = 2 * (5 * tl * kdim * LANES * 4 + 2 * tl * nv * LANES * 4 + 2 * kdim * nv * LANES * 4) \
        + kdim * nv * LANES * 4
    return pl.pallas_call(
        functools.partial(_rwkv_scan_kernel, tl=tl, nv=nv),
        grid=(nl // LANES, length // tl),
        in_specs=[kspec] * 5 + [vspec, sspec],
        out_specs=[vspec, sspec],
        out_shape=[jax.ShapeDtypeStruct((length, nv, nl), F32),
                   jax.ShapeDtypeStruct((kdim, nv, nl), F32)],
        scratch_shapes=[pltpu.VMEM((kdim, nv, LANES), F32)],
        compiler_params=_params(("parallel", "arbitrary"), est),
        name="rwkv_scan",
    )(*kvecs, vvec, s0)


def _rwkv_recurrence(r, w, k, v, kk, b, wkv0, bn, length):
    n = bn * A_HEADS
    dup = max(1, LANES // n)
    nv = A_HEAD_DIM // dup
    assert nv % SUBLANES == 0 and (n * dup) % LANES == 0, (bn, n)

    def to_t(x):
        x = x.reshape(bn, length, A_HEADS, A_HEAD_DIM)
        return jnp.transpose(x, (1, 3, 0, 2)).reshape(length, A_HEAD_DIM, n)

    def kpack(x):
        return jnp.concatenate([x] * dup, axis=-1) if dup > 1 else x

    def vpack(x):
        lead = x.shape[0]
        x = x.reshape(lead, dup, nv, n)
        return jnp.transpose(x, (0, 2, 1, 3)).reshape(lead, nv, dup * n)

    def vunpack(x):
        lead = x.shape[0]
        x = x.reshape(lead, nv, dup, n)
        return jnp.transpose(x, (0, 2, 1, 3)).reshape(lead, A_HEAD_DIM, n)

    kvecs = [kpack(to_t(x)) for x in (r, w, k, kk, b)]
    vvec = vpack(to_t(v))
    s0 = jnp.transpose(wkv0.astype(F32), (3, 2, 0, 1)).reshape(A_HEAD_DIM, A_HEAD_DIM, n)
    s0 = vpack(s0)
    tl = _tile(length, 32, 1)
    y_t, s_last = _rwkv_scan(kvecs, vvec, s0, tl)
    y = vunpack(y_t).reshape(length, A_HEAD_DIM, bn, A_HEADS)
    y = jnp.transpose(y, (2, 0, 3, 1)).reshape(bn * length, A_WIDTH)
    s_last = vunpack(s_last).reshape(A_HEAD_DIM, A_HEAD_DIM, bn, A_HEADS)
    s_last = jnp.transpose(s_last, (2, 3, 1, 0))
    return y, s_last


def _retention_kernel(p_ref, cos_ref, sin_ref, qdec_ref, kdec_ref, intra_ref, cdec_ref, s0_ref,
                      y_ref, sout_ref, s_scr, *, sb, c):
    ci = pl.program_id(1)

    @pl.when(ci == 0)
    def _():
        s_scr[...] = s0_ref[...]

    lane = lax.broadcasted_iota(jnp.int32, (c, 2 * B_QK_WIDTH), 1)
    first_half = (lane % B_QK_DIM) < (B_QK_DIM // 2)
    cosf = cos_ref[...]
    sinf = sin_ref[...]
    for s in range(sb):
        rows = slice(s * c, (s + 1) * c)
        qk = p_ref[rows, 0:2 * B_QK_WIDTH]
        half = B_QK_DIM // 2
        swapped = jnp.where(first_half, pltpu.roll(qk, 2 * B_QK_WIDTH - half, 1),
                            pltpu.roll(qk, half, 1))
        rot = qk * cosf + swapped * sinf
        q = rot[:, :B_QK_WIDTH] * (B_QK_DIM ** -0.5)
        k = rot[:, B_QK_WIDTH:]
        qd = (q * qdec_ref[...]).astype(BF16)
        kd = (k * kdec_ref[...]).astype(BF16)
        qb = q.astype(BF16)
        kb = k.astype(BF16)
        for h in range(B_HEADS):
            qs = slice(h * B_QK_DIM, (h + 1) * B_QK_DIM)
            vs = slice(2 * B_QK_WIDTH + h * B_V_DIM, 2 * B_QK_WIDTH + (h + 1) * B_V_DIM)
            gs = slice(2 * B_QK_WIDTH + B_V_WIDTH + h * B_V_DIM,
                       2 * B_QK_WIDTH + B_V_WIDTH + (h + 1) * B_V_DIM)
            vh = p_ref[rows, vs].astype(BF16)
            gh = p_ref[rows, gs]
            st = s_scr[s, h]
            scores = lax.dot_general(qb[:, qs], kb[:, qs], (((1,), (1,)), ((), ())),
                                     preferred_element_type=F32) * intra_ref[h]
            o = _dot(scores.astype(BF16), vh) + _dot(qd[:, qs], st.astype(BF16))
            s_scr[s, h] = st * cdec_ref[h] + lax.dot_general(
                kd[:, qs], vh, (((0,), (0,)), ((), ())), preferred_element_type=F32)
            o = o * lax.rsqrt(jnp.mean(o * o, -1, keepdims=True) + NORM_EPS)
            y_ref[rows, h * B_V_DIM:(h + 1) * B_V_DIM] = o * (gh * _sigmoid(gh))

    @pl.when(ci == pl.num_programs(1) - 1)
    def _():
        sout_ref[...] = s_scr[...]


def _retention(pb, s0, bn, length, pos0):
    c = math.gcd(length, RET_CHUNK)
    nc = length // c
    sb = 1 if nc > 1 else _tile(bn, 8, 1)
    half = B_QK_DIM // 2
    inv = ROPE_BASE ** (-jnp.arange(half, dtype=F32) / half)
    pos = (pos0 + jnp.arange(length)).astype(F32)
    ang = pos[:, None] * inv[None, :]
    cos, sin = jnp.cos(ang), jnp.sin(ang)
    cosf = jnp.tile(jnp.concatenate([cos, cos], -1), (1, 2 * B_HEADS))
    sinf = jnp.tile(jnp.concatenate([-sin, sin], -1), (1, 2 * B_HEADS))
    log_g = jnp.log1p(-jnp.exp2(-5.0 - jnp.arange(B_HEADS, dtype=F32)))
    idx = jnp.arange(c, dtype=F32)
    diff = idx[:, None] - idx[None, :]
    intra = jnp.where(diff >= 0, jnp.exp(jnp.maximum(diff, 0.0) * log_g[:, None, None]), 0.0)
    q_dec = jnp.exp((idx + 1.0)[:, None] * log_g[None, :])
    k_dec = jnp.exp((c - 1.0 - idx)[:, None] * log_g[None, :])
    c_dec = jnp.exp(c * log_g)
    qdec = jnp.repeat(q_dec, B_QK_DIM, axis=1)
    kdec = jnp.repeat(k_dec, B_QK_DIM, axis=1)
    cdec = jnp.broadcast_to(c_dec[:, None, None], (B_HEADS, 1, B_V_DIM))

    rows = sb * c
    full = lambda shape: pl.BlockSpec(shape, lambda i, j: (0,) * len(shape))
    sspec = pl.BlockSpec((sb, B_HEADS, B_QK_DIM, B_V_DIM), lambda i, j: (i, 0, 0, 0))
    est = 2 * (rows * B_COLS * 4 + rows * B_V_WIDTH * 4 + 2 * sb * B_HEADS * B_QK_DIM * B_V_DIM * 4) \
        + 3 * sb * B_HEADS * B_QK_DIM * B_V_DIM * 4 + 8 * rows * B_COLS * 4
    return pl.pallas_call(
        functools.partial(_retention_kernel, sb=sb, c=c),
        grid=(bn // sb, nc),
        in_specs=[pl.BlockSpec((rows, B_COLS), lambda i, j: (i * nc + j, 0)),
                  pl.BlockSpec((c, 2 * B_QK_WIDTH), lambda i, j: (j, 0)),
                  pl.BlockSpec((c, 2 * B_QK_WIDTH), lambda i, j: (j, 0)),
                  full((c, B_QK_WIDTH)), full((c, B_QK_WIDTH)), full((B_HEADS, c, c)),
                  full((B_HEADS, 1, B_V_DIM)), sspec],
        out_specs=[pl.BlockSpec((rows, B_V_WIDTH), lambda i, j: (i * nc + j, 0)), sspec],
        out_shape=[jax.ShapeDtypeStruct((bn * length, B_V_WIDTH), F32),
                   jax.ShapeDtypeStruct((bn, B_HEADS, B_QK_DIM, B_V_DIM), F32)],
        scratch_shapes=[pltpu.VMEM((sb, B_HEADS, B_QK_DIM, B_V_DIM), F32)],
        compiler_params=_params(("parallel", "arbitrary"), est),
        name="retention",
    )(pb, cosf, sinf, qdec, kdec, intra, cdec, s0.astype(F32))


def _mix_out_kernel(x_ref, y_ref, r_ref, k_ref, v_ref, g_ref, yb_ref, rk_ref, lg_ref, lb_ref,
                    ones_ref, w_ref, o_ref):
    ones = ones_ref[...]
    y = y_ref[...]
    inv_d = 1.0 / A_HEAD_DIM
    mean = _seg_sum(y, ones) * inv_d
    d = y - mean
    var = _seg_sum(d * d, ones) * inv_d
    yn = d * lax.rsqrt(var + A_GN_EPS) * lg_ref[...] + lb_ref[...]
    v = v_ref[...]
    bonus = _seg_sum(r_ref[...] * k_ref[...] * rk_ref[...], ones) * v
    ya = ((yn + bonus) * g_ref[...]).astype(BF16)
    o_ref[...] = (x_ref[...] + _dot(ya, w_ref[0:A_WIDTH, :])
                  + _dot(yb_ref[...].astype(BF16), w_ref[A_WIDTH:, :]))


def _mix_out(x, y, r, k, v, g, yb, r_k, lnx_g, lnx_b, ones, w_out):
    t, d = x.shape
    tm = _tile(t, 512)
    row = lambda n: pl.BlockSpec((tm, n), lambda i: (i, 0))
    full = lambda shape: pl.BlockSpec(shape, lambda i: (0,) * len(shape))
    est = 2 * (2 * tm * d * 4 + 6 * tm * A_WIDTH * 4 + (A_WIDTH + B_V_WIDTH) * d * 2) + 10 * tm * A_WIDTH * 4
    return pl.pallas_call(
        _mix_out_kernel,
        grid=(t // tm,),
        in_specs=[row(d)] + [row(A_WIDTH)] * 6 + [full((1, A_WIDTH))] * 3
                 + [full((A_WIDTH, A_WIDTH)), full((A_WIDTH + B_V_WIDTH, d))],
        out_specs=row(d),
        out_shape=jax.ShapeDtypeStruct((t, d), F32),
        compiler_params=_params(("parallel",), est),
        name="mix_out",
    )(x, y, r, k, v, g, yb, r_k.reshape(1, A_WIDTH), lnx_g.reshape(1, A_WIDTH),
      lnx_b.reshape(1, A_WIDTH), ones, w_out)


def _ffn_kernel(x_ref, g_ref, wg_ref, wu_ref, wd_ref, o_ref, h_scr):
    j = pl.program_id(1)

    @pl.when(j == 0)
    def _():
        x = x_ref[...]
        h_scr[...] = _rms(x, g_ref[...]).astype(BF16)
        o_ref[...] = x

    h = h_scr[...]
    a = _dot(h, wg_ref[...])
    b = _dot(h, wu_ref[...])
    m = (a * _sigmoid(a) * b).astype(BF16)
    o_ref[...] += _dot(m, wd_ref[...])


def _ffn(x, g, wg, wu, wd):
    t, d = x.shape
    ff = wg.shape[1]
    tm = _tile(t, 512)
    tf = _tile(ff, 1536, LANES)
    est = 2 * (2 * tm * d * 4 + 3 * d * tf * 2) + tm * d * 2 + 3 * tm * tf * 4
    return pl.pallas_call(
        _ffn_kernel,
        grid=(t // tm, ff // tf),
        in_specs=[pl.BlockSpec((tm, d), lambda i, j: (i, 0)),
                  pl.BlockSpec((1, d), lambda i, j: (0, 0)),
                  pl.BlockSpec((d, tf), lambda i, j: (0, j)),
                  pl.BlockSpec((d, tf), lambda i, j: (0, j)),
                  pl.BlockSpec((tf, d), lambda i, j: (j, 0))],
        out_specs=pl.BlockSpec((tm, d), lambda i, j: (i, 0)),
        out_shape=jax.ShapeDtypeStruct((t, d), F32),
        scratch_shapes=[pltpu.VMEM((tm, d), BF16)],
        compiler_params=_params(("parallel", "arbitrary"), est),
        name="ffn",
    )(x, g.reshape(1, d), wg, wu, wd)


def _rglru_kernel(gate_ref, xb_ref, tail0_ref, h0_ref, cw_ref, cb_ref, wa_ref, ba_ref, wx_ref,
                  bx_ref, lam_ref, y_ref, hlast_ref, tail_scr, h_scr, a_scr, b_scr, hs_scr, *, tl):
    tb = pl.program_id(1)

    @pl.when(tb == 0)
    def _():
        tail_scr[...] = tail0_ref[0]
        h_scr[...] = h0_ref[0]

    xb = xb_ref[...]
    full = jnp.concatenate([tail_scr[...], xb], axis=0)
    xc = cb_ref[...]
    for j in range(CONV_W):
        shift = CONV_W - 1 - j
        term = pltpu.roll(full, shift, 0) if shift else full
        xc = xc + term[SUBLANES:, :] * cw_ref[j:j + 1, :]
    tail_scr[...] = xb[tl - SUBLANES:, :]
    xcb = xc.astype(BF16)
    for n in range(C_BLOCKS):
        sl = slice(n * C_BLOCK_DIM, (n + 1) * C_BLOCK_DIM)
        xn = xcb[:, sl]
        r = _sigmoid(_dot(xn, wa_ref[n]) + ba_ref[:, sl])
        i = _sigmoid(_dot(xn, wx_ref[n]) + bx_ref[:, sl])
        lam = lam_ref[:, sl]
        softplus_neg_lam = jnp.maximum(-lam, 0.0) + jnp.log1p(jnp.exp(-jnp.abs(lam)))
        log_a = -LRU_C * r * softplus_neg_lam
        a = jnp.exp(log_a)
        gain = jnp.sqrt(-jnp.tanh(log_a) * (a * a + 1.0))
        a_scr[:, sl] = a
        b_scr[:, sl] = gain * i * xc[:, sl]

    def row(t, h):
        h = a_scr[pl.ds(t, 1), :] * h + b_scr[pl.ds(t, 1), :]
        hs_scr[pl.ds(t, 1), :] = h
        return h

    h = lax.fori_loop(0, tl, row, h_scr[...], unroll=8)
    h_scr[...] = h
    hlast_ref[0] = h
    gate = gate_ref[...]
    cdf = 0.5 * (1.0 + jnp.tanh(math.sqrt(2.0 / math.pi) * (gate + 0.044715 * (gate * gate * gate))))
    y_ref[...] = (gate * cdf * hs_scr[...]).astype(BF16)


def _rglru(gate, xb, tail0, h0, wts, bn, length):
    tl = _tile(length, 256)
    nt = length // tl
    full = lambda shape: pl.BlockSpec(shape, lambda i, j: (0,) * len(shape))
    row = pl.BlockSpec((tl, D_RNN), lambda i, j: (i * nt + j, 0))
    est = 2 * (3 * tl * D_RNN * 4 + 2 * C_BLOCKS * C_BLOCK_DIM * C_BLOCK_DIM * 2) + 10 * tl * D_RNN * 4
    return pl.pallas_call(
        functools.partial(_rglru_kernel, tl=tl),
        grid=(bn, nt),
        in_specs=[row, row,
                  pl.BlockSpec((1, SUBLANES, D_RNN), lambda i, j: (i, 0, 0)),
                  pl.BlockSpec((1, 1, D_RNN), lambda i, j: (i, 0, 0)),
                  full((CONV_W, D_RNN)), full((1, D_RNN)),
                  full((C_BLOCKS, C_BLOCK_DIM, C_BLOCK_DIM)), full((1, D_RNN)),
                  full((C_BLOCKS, C_BLOCK_DIM, C_BLOCK_DIM)), full((1, D_RNN)),
                  full((1, D_RNN))],
        out_specs=[row, pl.BlockSpec((1, 1, D_RNN), lambda i, j: (i, 0, 0))],
        out_shape=[jax.ShapeDtypeStruct((bn * length, D_RNN), BF16),
                   jax.ShapeDtypeStruct((bn, 1, D_RNN), F32)],
        scratch_shapes=[pltpu.VMEM((SUBLANES, D_RNN), F32), pltpu.VMEM((1, D_RNN), F32),
                        pltpu.VMEM((tl, D_RNN), F32), pltpu.VMEM((tl, D_RNN), F32),
                        pltpu.VMEM((tl, D_RNN), F32)],
        compiler_params=_params(("parallel", "arbitrary"), est),
        name="rglru",
    )(gate, xb, tail0, h0, *wts)


def _matmul_res_kernel(y_ref, w_ref, x_ref, o_ref):
    o_ref[...] = x_ref[...] + _dot(y_ref[...], w_ref[...])


def _matmul_res(y, w, x):
    t, d = x.shape
    kdim = y.shape[1]
    tm = _tile(t, 512, 16)
    est = 2 * (tm * kdim * 2 + kdim * d * 2 + 2 * tm * d * 4)
    return pl.pallas_call(
        _matmul_res_kernel,
        grid=(t // tm,),
        in_specs=[pl.BlockSpec((tm, kdim), lambda i: (i, 0)),
                  pl.BlockSpec((kdim, d), lambda i: (0, 0)),
                  pl.BlockSpec((tm, d), lambda i: (i, 0))],
        out_specs=pl.BlockSpec((tm, d), lambda i: (i, 0)),
        out_shape=jax.ShapeDtypeStruct((t, d), F32),
        compiler_params=_params(("parallel",), est),
        name="matmul_res",
    )(y, w, x)


def _router_kernel(x_ref, g_ref, wr_ref, h_ref, route_ref, *, tm):
    h = _rms(x_ref[...], g_ref[...])
    h_ref[...] = h
    logits = jnp.dot(h, wr_ref[...], preferred_element_type=F32, precision=lax.Precision.HIGHEST)
    lane = lax.broadcasted_iota(jnp.int32, (tm, LANES), 1).astype(F32)
    neg = jnp.float32(-jnp.inf)
    lg = jnp.where(lane < N_EXPERTS, logits, neg)
    m1 = jnp.max(lg, -1, keepdims=True)
    i1 = jnp.min(jnp.where(lg == m1, lane, float(LANES)), -1, keepdims=True)
    lg2 = jnp.where(lane == i1, neg, lg)
    m2 = jnp.max(lg2, -1, keepdims=True)
    i2 = jnp.min(jnp.where(lg2 == m2, lane, float(LANES)), -1, keepdims=True)
    e = jnp.exp(m2 - m1)
    g1 = 1.0 / (1.0 + e)
    g2 = e / (1.0 + e)
    route_ref[...] = jnp.where(lane == 0, i1, jnp.where(lane == 1, i2, jnp.where(
        lane == 2, g1, jnp.where(lane == 3, g2, 0.0))))


def _router(x, g, wr_pad):
    t, d = x.shape
    tm = _tile(t, 512)
    est = 2 * (2 * tm * d * 4 + d * LANES * 4 + tm * LANES * 4) + 2 * tm * d * 4
    return pl.pallas_call(
        functools.partial(_router_kernel, tm=tm),
        grid=(t // tm,),
        in_specs=[pl.BlockSpec((tm, d), lambda i: (i, 0)),
                  pl.BlockSpec((1, d), lambda i: (0, 0)),
                  pl.BlockSpec((d, LANES), lambda i: (0, 0))],
        out_specs=[pl.BlockSpec((tm, d), lambda i: (i, 0)),
                   pl.BlockSpec((tm, LANES), lambda i: (i, 0))],
        out_shape=[jax.ShapeDtypeStruct((t, d), F32), jax.ShapeDtypeStruct((t, LANES), F32)],
        compiler_params=_params(("parallel",), est),
        name="router",
    )(x, g.reshape(1, d), wr_pad)


def _moe_kernel(be_ref, bv_ref, tok_hbm, h_hbm, gate_ref, wg_ref, wu_ref, wd_ref, o_ref,
                idx_smem, xbuf, xbf, sem_idx, sem_rows, *, tm):
    i = pl.program_id(0)
    j = pl.program_id(1)
    valid = bv_ref[i] != 0

    def row_copy(tok, r):
        return pltpu.make_async_copy(h_hbm.at[pl.ds(tok, 1)], xbuf.at[pl.ds(r, 1)], sem_rows)

    @pl.when(jnp.logical_and(valid, j == 0))
    def _():
        idx_copy = pltpu.make_async_copy(tok_hbm.at[pl.ds(i * tm, tm)], idx_smem, sem_idx)
        idx_copy.start()
        idx_copy.wait()

        def issue(r, c):
            row_copy(idx_smem[r], r).start()
            return c

        lax.fori_loop(0, tm, issue, 0, unroll=8)

        def drain(r, c):
            row_copy(0, r).wait()
            return c

        lax.fori_loop(0, tm, drain, 0, unroll=8)
        xbf[...] = xbuf[...].astype(BF16)

    @pl.when(valid)
    def _():
        x = xbf[...]
        a = _dot(x, wg_ref[...])
        b = _dot(x, wu_ref[...])
        m = (a * _sigmoid(a) * b).astype(BF16)
        contrib = _dot(m, wd_ref[...])

        @pl.when(j == 0)
        def _():
            o_ref[...] = contrib

        @pl.when(j > 0)
        def _():
            o_ref[...] += contrib

        @pl.when(j == pl.num_programs(1) - 1)
        def _():
            o_ref[...] = o_ref[...] * gate_ref[...]

    @pl.when(jnp.logical_and(jnp.logical_not(valid), j == 0))
    def _():
        o_ref[...] = jnp.zeros_like(o_ref)


def _moe(block_e, block_valid, tok_sorted, h, gate_sorted, wg, wu, wd, tm):
    p = tok_sorted.shape[0]
    d = h.shape[1]
    ff = wg.shape[2]
    tf = _tile(ff, 1792, LANES)
    nf = ff // tf
    nb = p // tm

    def wcol(i, j, be, bv):
        return (be[i], 0, jnp.where(bv[i] != 0, j, nf - 1))

    def wrow(i, j, be, bv):
        return (be[i], jnp.where(bv[i] != 0, j, nf - 1), 0)

    est = 2 * (3 * d * tf * 2 + tm * d * 4 + tm * LANES * 4) + tm * d * 6 + 3 * tm * tf * 4
    grid_spec = pltpu.PrefetchScalarGridSpec(
        num_scalar_prefetch=2,
        grid=(nb, nf),
        in_specs=[pl.BlockSpec(memory_space=pl.ANY),
                  pl.BlockSpec(memory_space=pl.ANY),
                  pl.BlockSpec((tm, 1), lambda i, j, be, bv: (i, 0)),
                  pl.BlockSpec((None, d, tf), wcol),
                  pl.BlockSpec((None, d, tf), wcol),
                  pl.BlockSpec((None, tf, d), wrow)],
        out_specs=pl.BlockSpec((tm, d), lambda i, j, be, bv: (i, 0)),
        scratch_shapes=[pltpu.SMEM((tm,), jnp.int32), pltpu.VMEM((tm, d), F32),
                        pltpu.VMEM((tm, d), BF16), pltpu.SemaphoreType.DMA,
                        pltpu.SemaphoreType.DMA],
    )
    return pl.pallas_call(
        functools.partial(_moe_kernel, tm=tm),
        grid_spec=grid_spec,
        out_shape=jax.ShapeDtypeStruct((p, d), F32),
        compiler_params=_params(("arbitrary", "arbitrary"), est),
        name="moe",
    )(block_e, block_valid, tok_sorted, h, gate_sorted, wg, wu, wd)


def _combine_kernel(pos_hbm, y_hbm, x_ref, g_ref, o_ref, idx_smem, ybuf, sem_idx, sem_rows, *, tm):
    i = pl.program_id(0)
    idx_copy = pltpu.make_async_copy(pos_hbm.at[pl.ds(i * 2 * tm, 2 * tm)], idx_smem, sem_idx)
    idx_copy.start()
    idx_copy.wait()

    def row_copy(src, r):
        return pltpu.make_async_copy(y_hbm.at[pl.ds(src, 1)], ybuf.at[pl.ds(r, 1)], sem_rows)

    def issue(r, c):
        row_copy(idx_smem[r], r).start()
        return c

    lax.fori_loop(0, 2 * tm, issue, 0, unroll=8)

    def drain(r, c):
        row_copy(0, r).wait()
        return c

    lax.fori_loop(0, 2 * tm, drain, 0, unroll=8)
    x = x_ref[...] + ybuf[0:tm, :] + ybuf[tm:2 * tm, :]
    o_ref[...] = _rms(x, g_ref[...])


def _combine(pos, y_sorted, x, g):
    t, d = x.shape
    tm = pos.shape[0] // (2 * (t // _tile(t, 256)))
    assert tm == _tile(t, 256)
    est = 2 * (2 * tm * d * 4) + 2 * tm * d * 4 + 2 * tm * d * 4
    return pl.pallas_call(
        functools.partial(_combine_kernel, tm=tm),
        grid=(t // tm,),
        in_specs=[pl.BlockSpec(memory_space=pl.ANY),
                  pl.BlockSpec(memory_space=pl.ANY),
                  pl.BlockSpec((tm, d), lambda i: (i, 0)),
                  pl.BlockSpec((1, d), lambda i: (0, 0))],
        out_specs=pl.BlockSpec((tm, d), lambda i: (i, 0)),
        out_shape=jax.ShapeDtypeStruct((t, d), F32),
        scratch_shapes=[pltpu.SMEM((2 * tm,), jnp.int32), pltpu.VMEM((2 * tm, d), F32),
                        pltpu.SemaphoreType.DMA, pltpu.SemaphoreType.DMA],
        compiler_params=_params(("arbitrary",), est),
        name="combine",
    )(pos, y_sorted, x, g.reshape(1, d))


def _route_plan(route, tm):
    t = route.shape[0]
    flat_e = route[:, 0:TOP_K].astype(jnp.int32).reshape(-1)
    flat_gate = route[:, TOP_K:2 * TOP_K].reshape(-1)
    flat_tok = jnp.repeat(jnp.arange(t, dtype=jnp.int32), TOP_K)
    onehot = (flat_e[:, None] == jnp.arange(N_EXPERTS, dtype=jnp.int32)[None, :]).astype(jnp.int32)
    csum = jnp.cumsum(onehot, axis=0)
    counts = csum[-1]
    rank = jnp.sum((csum - onehot) * onehot, axis=1)
    padded = ((counts + tm - 1) // tm) * tm
    pend = jnp.cumsum(padded)
    pstart = pend - padded
    dest = pstart[flat_e] + rank
    nb = (t * TOP_K + tm - 1) // tm + N_EXPERTS
    p = nb * tm
    tok_sorted = jnp.zeros((p,), jnp.int32).at[dest].set(flat_tok)
    gate_sorted = jnp.zeros((p,), F32).at[dest].set(flat_gate)
    block_start = jnp.arange(nb, dtype=jnp.int32) * tm
    block_valid = (block_start < pend[-1]).astype(jnp.int32)
    last_e = jnp.searchsorted(pend, pend[-1] - 1, side='right').astype(jnp.int32)
    block_e = jnp.searchsorted(pend, block_start, side='right').astype(jnp.int32)
    block_e = jnp.where(block_valid != 0, jnp.minimum(block_e, N_EXPERTS - 1), last_e)
    return tok_sorted, gate_sorted.reshape(p, 1), block_e, block_valid, dest.reshape(t, TOP_K)


def _tile_pos(dest, tm):
    t = dest.shape[0]
    return jnp.transpose(dest.reshape(t // tm, tm, TOP_K), (0, 2, 1)).reshape(-1)


def _prep_weights(w):
    bf = lambda a: a.astype(BF16)
    ones = (jnp.arange(A_WIDTH)[:, None] // A_HEAD_DIM == jnp.arange(A_WIDTH)[None, :] // A_HEAD_DIM)
    lora = jnp.zeros((A_LORA_COLS, A_WIDTH), F32)
    out = dict(w)
    out['ones'] = ones.astype(BF16)
    out['wdec_pad'] = bf(lora.at[0:64].set(w['a_w_decay'][0]))
    out['wiclr_pad'] = bf(lora.at[64:128].set(w['a_w_iclr'][0]))
    out['wgate_pad'] = bf(lora.at[128:256].set(w['a_w_gate'][0]))
    for name in ('w_in0', 'w_out0', 'ffn_gate', 'ffn_up', 'ffn_down', 'w_in1', 'w_out1', 'c_w_a',
                 'c_w_x', 'moe_gate', 'moe_up', 'moe_down'):
        out[name] = bf(w[name][0])
    out['router_pad'] = jnp.zeros((w['moe_router'].shape[1], LANES), F32).at[:, :N_EXPERTS].set(
        w['moe_router'][0])
    return out


def _layer0(x, bn, length, pos0, shift, wkv, ret, w):
    t = bn * length
    pa, pb = _norm_matmul(x, w['norm_mix0'][0], w['w_in0'], (A_COLS, B_COLS))
    tm = _tile(t, 256) if length >= 256 else _tile(t, 256, length)
    if length >= tm:
        starts = jnp.arange(t // tm) * tm
        before = pa[jnp.maximum(starts - 1, 0)]
        first = jnp.where((starts % length == 0)[:, None], shift.astype(F32)[starts // length], before)
        ext = jnp.zeros((t // tm, SUBLANES, A_COLS), F32).at[:, 0].set(first).reshape(-1, A_COLS)
    else:
        ext = jnp.repeat(shift.astype(F32), length, axis=0)
    row = lambda a: a[0].reshape(1, -1)
    r, dec, kmod, v, kkn, bneg, g = _rwkv_prep(
        pa, ext, length, tm,
        (row(w['a_mu']), row(w['a_w0']), w['wdec_pad'], row(w['a_a0']), w['wiclr_pad'],
         w['wgate_pad'], row(w['a_k_k']), row(w['a_k_a']), w['ones']))
    y, wkv_new = _rwkv_recurrence(r, dec, kmod, v, kkn, bneg, wkv, bn, length)
    yb, ret_new = _retention(pb, ret, bn, length, pos0)
    x = _mix_out(x, y, r, kmod, v, g, yb, w['a_r_k'][0].reshape(-1), w['a_lnx_g'][0], w['a_lnx_b'][0],
                 w['ones'], w['w_out0'])
    x = _ffn(x, w['norm_ffn0'][0], w['ffn_gate'], w['ffn_up'], w['ffn_down'])
    shift_new = pa.reshape(bn, length, A_COLS)[:, -1]
    return x, shift_new, wkv_new, ret_new


def _layer1_mixer(x, bn, length, conv, hlru, w):
    assert length >= CONV_W - 1
    gate, xb = _norm_matmul(x, w['norm_mix1'][0], w['w_in1'], (D_RNN, D_RNN))
    tail0 = jnp.zeros((bn, SUBLANES, D_RNN), F32).at[:, SUBLANES - (CONV_W - 1):].set(conv.astype(F32))
    row = lambda a: a[0].reshape(1, -1)
    y, h_last = _rglru(gate, xb, tail0, hlru.astype(F32).reshape(bn, 1, D_RNN),
                       (w['c_conv_w'][0], row(w['c_conv_b']), w['c_w_a'], row(w['c_b_a']),
                        w['c_w_x'], row(w['c_b_x']), row(w['c_lambda'])), bn, length)
    x = _matmul_res(y, w['w_out1'], x)
    conv_new = xb.reshape(bn, length, D_RNN)[:, length - (CONV_W - 1):]
    return x, conv_new, h_last.reshape(bn, D_RNN)


def kernel(x_prompt, x_sample, state_rwkv_shift, state_rwkv_wkv, state_ret, state_lru_conv, state_lru_h, norm_mix0, w_in0, a_mu, a_w0, a_w_decay, a_a0, a_w_iclr, a_w_gate, a_k_k, a_k_a, a_r_k, a_lnx_g, a_lnx_b, w_out0, norm_ffn0, ffn_gate, ffn_up, ffn_down, norm_mix1, w_in1, c_conv_w, c_conv_b, c_w_a, c_b_a, c_w_x, c_b_x, c_lambda, w_out1, norm_ffn1, moe_router, moe_gate, moe_up, moe_down, norm_final):
    w = _prep_weights(dict(
        norm_mix0=norm_mix0, w_in0=w_in0, a_mu=a_mu, a_w0=a_w0, a_w_decay=a_w_decay, a_a0=a_a0,
        a_w_iclr=a_w_iclr, a_w_gate=a_w_gate, a_k_k=a_k_k, a_k_a=a_k_a, a_r_k=a_r_k, a_lnx_g=a_lnx_g,
        a_lnx_b=a_lnx_b, w_out0=w_out0, norm_ffn0=norm_ffn0, ffn_gate=ffn_gate, ffn_up=ffn_up,
        ffn_down=ffn_down, norm_mix1=norm_mix1, w_in1=w_in1, c_conv_w=c_conv_w, c_conv_b=c_conv_b,
        c_w_a=c_w_a, c_b_a=c_b_a, c_w_x=c_w_x, c_b_x=c_b_x, c_lambda=c_lambda, w_out1=w_out1,
        norm_ffn1=norm_ffn1, moe_router=moe_router, moe_gate=moe_gate, moe_up=moe_up,
        moe_down=moe_down))
    dt = x_prompt.dtype
    d = x_prompt.shape[-1]
    bp, lp = x_prompt.shape[:2]
    bs, ls = x_sample.shape[:2]
    zeros = lambda shape: jnp.zeros(shape, F32)
    groups = [
        (x_prompt.reshape(bp * lp, d), bp, lp, 0, zeros((bp, A_COLS)),
         zeros((bp, A_HEADS, A_HEAD_DIM, A_HEAD_DIM)), zeros((bp, B_HEADS, B_QK_DIM, B_V_DIM)),
         zeros((bp, CONV_W - 1, D_RNN)), zeros((bp, D_RNN))),
        (x_sample.reshape(bs * ls, d), bs, ls, PAST_LEN, state_rwkv_shift[0], state_rwkv_wkv[0],
         state_ret[0], state_lru_conv[0], state_lru_h[0]),
    ]
    xs, states, hs, routes = [], [], [], []
    for x, bn, length, pos0, shift, wkv, ret, conv, hlru in groups:
        x, s_shift, s_wkv, s_ret = _layer0(x, bn, length, pos0, shift, wkv, ret, w)
        x, s_conv, s_h = _layer1_mixer(x, bn, length, conv, hlru, w)
        h, route = _router(x, w['norm_ffn1'][0], w['router_pad'])
        xs.append(x)
        hs.append(h)
        routes.append(route)
        states.append((s_shift, s_wkv, s_ret, s_conv, s_h))
    h_all = jnp.concatenate(hs, 0)
    route_all = jnp.concatenate(routes, 0)
    tm_moe = 512 if h_all.shape[0] >= 4096 else 64
    tok_sorted, gate_sorted, block_e, block_valid, dest = _route_plan(route_all, tm_moe)
    y_sorted = _moe(block_e, block_valid, tok_sorted, h_all, gate_sorted, w['moe_gate'], w['moe_up'],
                    w['moe_down'], tm_moe)
    outs = []
    off = 0
    for x in xs:
        t = x.shape[0]
        pos = _tile_pos(dest[off:off + t], _tile(t, 256))
        outs.append(_combine(pos, y_sorted, x, norm_final))
        off += t
    y_prompt = outs[0].reshape(bp, lp, d)
    y_sample = outs[1].reshape(bs, ls, d)
    st_p = tuple(s[None].astype(dt) for s in states[0])
    st_s = tuple(s[None].astype(dt) for s in states[1])
    return (y_prompt, y_sample) + st_p + st_s
```

```python
import functools
import math

import jax
import jax.numpy as jnp
from jax import lax
from jax.experimental import pallas as pl
from jax.experimental.pallas import tpu as pltpu

F32 = jnp.float32
BF16 = jnp.bfloat16

A_HEADS = 8
A_HEAD_DIM = 64
A_WIDTH = A_HEADS * A_HEAD_DIM
A_LORA_COLS = 256
A_COLS = 3 * A_WIDTH + A_LORA_COLS
A_GN_EPS = A_HEAD_DIM * 1e-5
B_HEADS = 4
B_QK_DIM = 64
B_V_DIM = 128
B_QK_WIDTH = B_HEADS * B_QK_DIM
B_V_WIDTH = B_HEADS * B_V_DIM
B_COLS = 2 * B_QK_WIDTH + 2 * B_V_WIDTH
RET_CHUNK = 64
ROPE_BASE = 10000.0
D_RNN = 1280
C_BLOCKS = 10
C_BLOCK_DIM = D_RNN // C_BLOCKS
CONV_W = 4
LRU_C = 8.0
N_EXPERTS = 8
TOP_K = 2
NORM_EPS = 1e-6
PAST_LEN = 16384

LANES = 128
SUBLANES = 8
VMEM_BUDGET_BYTES = 56 * 1024 * 1024


def _tile(n, pref, mult=SUBLANES):
    t = min(pref, n)
    while t > mult and (n % t or t % mult):
        t -= 1
    assert n % t == 0 and t % mult == 0, (n, pref, mult)
    return t


def _params(sem, est_bytes):
    limit = int(min(max(est_bytes * 5 // 4 + (4 << 20), 32 << 20), VMEM_BUDGET_BYTES))
    return pltpu.CompilerParams(dimension_semantics=sem, vmem_limit_bytes=limit)


def _rms(x, g):
    return x * lax.rsqrt(jnp.mean(x * x, -1, keepdims=True) + NORM_EPS) * g


def _dot(a, b):
    return jnp.dot(a, b, preferred_element_type=F32)


def _sigmoid(x):
    return 1.0 / (1.0 + jnp.exp(-x))


def _seg_sum(x, ones):
    hi = x.astype(BF16)
    lo = (x - hi.astype(F32)).astype(BF16)
    return _dot(hi, ones) + _dot(lo, ones)


def _norm_matmul_kernel(x_ref, g_ref, w_ref, *o_refs, splits):
    h = _rms(x_ref[...], g_ref[...]).astype(BF16)
    off = 0
    for o_ref, n in zip(o_refs, splits):
        o_ref[...] = _dot(h, w_ref[:, off:off + n])
        off += n


def _norm_matmul(x, g, w, splits):
    t, d = x.shape
    n = w.shape[1]
    tm = _tile(t, 512)
    est = 2 * (tm * d * 4 + d * n * 2 + tm * n * 4) + tm * n * 4
    return pl.pallas_call(
        functools.partial(_norm_matmul_kernel, splits=splits),
        grid=(t // tm,),
        in_specs=[pl.BlockSpec((tm, d), lambda i: (i, 0)),
                  pl.BlockSpec((1, d), lambda i: (0, 0)),
                  pl.BlockSpec((d, n), lambda i: (0, 0))],
        out_specs=[pl.BlockSpec((tm, s), lambda i: (i, 0)) for s in splits],
        out_shape=[jax.ShapeDtypeStruct((t, s), F32) for s in splits],
        compiler_params=_params(("arbitrary",), est),
        name="norm_matmul",
    )(x, g.reshape(1, d), w)


def _rwkv_prep_kernel(p_ref, ext_ref, mu_ref, w0_ref, wdec_ref, a0_ref, wiclr_ref, wgate_ref,
                      r_o, w_o, k_o, a_o, v_o, g_o, *, seq_len, tm):
    p = p_ref[...]
    rolled = pltpu.roll(p, 1, 0)
    row = lax.broadcasted_iota(jnp.int32, (tm, 1), 0)
    if seq_len >= tm:
        prev = jnp.where(row == 0, ext_ref[0:1, :], rolled)
    else:
        prev = jnp.where(row % seq_len == 0, ext_ref[...], rolled)
    pm = p + (prev - p) * mu_ref[...]
    r = pm[:, 0:A_WIDTH]
    k = pm[:, A_WIDTH:2 * A_WIDTH]
    v = pm[:, 2 * A_WIDTH:3 * A_WIDTH]
    tail = pm[:, 3 * A_WIDTH:A_COLS]
    w_pre = w0_ref[...] + _dot(jnp.tanh(tail).astype(BF16), wdec_ref[...])
    w_log = jnp.minimum(w_pre, 0.0) - jnp.log1p(jnp.exp(-jnp.abs(w_pre))) - 0.5
    decay = jnp.exp(-jnp.exp(w_log))
    a = _sigmoid(a0_ref[...] + _dot(tail.astype(BF16), wiclr_ref[...]))
    g = _dot(_sigmoid(tail).astype(BF16), wgate_ref[...])
    r_o[...] = r
    w_o[...] = decay
    k_o[...] = k
    a_o[...] = a
    v_o[...] = v
    g_o[...] = g


def _rwkv_prep(pa, ext, seq_len, tm, wts):
    t = pa.shape[0]
    full = lambda shape: pl.BlockSpec(shape, lambda i: (0,) * len(shape))
    ext_rows = SUBLANES if seq_len >= tm else tm
    est = 2 * (tm * A_COLS * 4 * 2 + 6 * tm * A_WIDTH * 4) + 12 * tm * A_WIDTH * 4
    return pl.pallas_call(
        functools.partial(_rwkv_prep_kernel, seq_len=seq_len, tm=tm),
        grid=(t // tm,),
        in_specs=[pl.BlockSpec((tm, A_COLS), lambda i: (i, 0)),
                  pl.BlockSpec((ext_rows, A_COLS), lambda i: (i, 0)),
                  full((1, A_COLS)), full((1, A_WIDTH)), full((A_LORA_COLS, A_WIDTH)),
                  full((1, A_WIDTH)), full((A_LORA_COLS, A_WIDTH)), full((A_LORA_COLS, A_WIDTH))],
        out_specs=[pl.BlockSpec((tm, A_WIDTH), lambda i: (i, 0))] * 6,
        out_shape=[jax.ShapeDtypeStruct((t, A_WIDTH), F32)] * 6,
        compiler_params=_params(("arbitrary",), est),
        name="rwkv_prep",
    )(pa, ext, *wts)


def _rwkv_scan_kernel(r_in, w_in, k_in, a_in, v_ref, kkp_ref, kap_ref, s0_ref, y_ref, sout_ref,
                      s_scr, r_ref, w_ref, k_ref, kk_ref, b_ref, *, tl, nv, dup):
    tb = pl.program_id(1)
    nvg = nv // SUBLANES
    kdim = A_HEAD_DIM

    @pl.when(tb == 0)
    def _():
        s_scr[...] = s0_ref[...]

    def lanes(x):
        return jnp.concatenate([x] * dup, axis=-1) if dup > 1 else x

    k = k_in[...]
    a = a_in[...]
    kk = k * kkp_ref[...]
    kkn = kk / jnp.maximum(jnp.sqrt(jnp.sum(kk * kk, axis=1, keepdims=True)), 1e-12)
    r_ref[...] = lanes(r_in[...])
    w_ref[...] = lanes(w_in[...])
    k_ref[...] = lanes(k * (1.0 + (a - 1.0) * kap_ref[...]))
    kk_ref[...] = lanes(kkn)
    b_ref[...] = lanes(-(kkn * a))

    def bcast(ref, t, k):
        return jnp.broadcast_to(ref[t, pl.ds(k, 1), :], (SUBLANES, LANES))

    def step(t, carry):
        sa = [jnp.zeros((SUBLANES, LANES), F32) for _ in range(nvg)]
        for k in range(kdim):
            kk_row = bcast(kk_ref, t, k)
            for vg in range(nvg):
                sa[vg] = sa[vg] + s_scr[k, vg * SUBLANES:(vg + 1) * SUBLANES, :] * kk_row
        vv = [v_ref[t, vg * SUBLANES:(vg + 1) * SUBLANES, :] for vg in range(nvg)]
        y = [jnp.zeros((SUBLANES, LANES), F32) for _ in range(nvg)]
        for k in range(kdim):
            w_row = bcast(w_ref, t, k)
            b_row = bcast(b_ref, t, k)
            k_row = bcast(k_ref, t, k)
            r_row = bcast(r_ref, t, k)
            for vg in range(nvg):
                rows = slice(vg * SUBLANES, (vg + 1) * SUBLANES)
                s_new = s_scr[k, rows, :] * w_row + sa[vg] * b_row + vv[vg] * k_row
                s_scr[k, rows, :] = s_new
                y[vg] = y[vg] + s_new * r_row
        for vg in range(nvg):
            y_ref[t, vg * SUBLANES:(vg + 1) * SUBLANES, :] = y[vg]
        return carry

    lax.fori_loop(0, tl, step, 0)

    @pl.when(tb == pl.num_programs(1) - 1)
    def _():
        sout_ref[...] = s_scr[...]


def _rwkv_scan(kvecs, vvec, kkp, kap, s0, tl, dup):
    length, kdim, n = kvecs[0].shape
    nl = n * dup
    nbk = LANES // dup
    nv = vvec.shape[1]
    kspec = pl.BlockSpec((tl, kdim, nbk), lambda g, tb: (tb, 0, g))
    pspec = pl.BlockSpec((kdim, nbk), lambda g, tb: (0, g))
    vspec = pl.BlockSpec((tl, nv, LANES), lambda g, tb: (tb, 0, g))
    sspec = pl.BlockSpec((kdim, nv, LANES), lambda g, tb: (0, 0, g))
    est = 2 * (4 * tl * kdim * LANES * 4 + 2 * tl * nv * LANES * 4 + 2 * kdim * nv * LANES * 4) \
        + kdim * nv * LANES * 4 + 9 * tl * kdim * LANES * 4
    return pl.pallas_call(
        functools.partial(_rwkv_scan_kernel, tl=tl, nv=nv, dup=dup),
        grid=(nl // LANES, length // tl),
        in_specs=[kspec] * 4 + [vspec, pspec, pspec, sspec],
        out_specs=[vspec, sspec],
        out_shape=[jax.ShapeDtypeStruct((length, nv, nl), F32),
                   jax.ShapeDtypeStruct((kdim, nv, nl), F32)],
        scratch_shapes=[pltpu.VMEM((kdim, nv, LANES), F32)]
                       + [pltpu.VMEM((tl, kdim, LANES), F32)] * 5,
        compiler_params=_params(("arbitrary", "arbitrary"), est),
        name="rwkv_scan",
    )(*kvecs, vvec, kkp, kap, s0)


def _rwkv_recurrence(r, w, k, a, v, k_k, k_a, wkv0, bn, length):
    n = bn * A_HEADS
    dup = max(1, LANES // n)
    nv = A_HEAD_DIM // dup
    assert nv % SUBLANES == 0 and (n * dup) % LANES == 0, (bn, n)

    def to_t(x):
        x = x.reshape(bn, length, A_HEADS, A_HEAD_DIM)
        return jnp.transpose(x, (1, 3, 0, 2)).reshape(length, A_HEAD_DIM, n)

    def param_t(p):
        p = jnp.broadcast_to(p.reshape(1, A_HEADS, A_HEAD_DIM), (bn, A_HEADS, A_HEAD_DIM))
        return jnp.transpose(p, (2, 0, 1)).reshape(A_HEAD_DIM, n).astype(F32)

    def vpack(x):
        lead = x.shape[0]
        x = x.reshape(lead, dup, nv, n)
        return jnp.transpose(x, (0, 2, 1, 3)).reshape(lead, nv, dup * n)

    def vunpack(x):
        lead = x.shape[0]
        x = x.reshape(lead, nv, dup, n)
        return jnp.transpose(x, (0, 2, 1, 3)).reshape(lead, A_HEAD_DIM, n)

    kvecs = [to_t(x) for x in (r, w, k, a)]
    vvec = vpack(to_t(v))
    s0 = jnp.transpose(wkv0.astype(F32), (3, 2, 0, 1)).reshape(A_HEAD_DIM, A_HEAD_DIM, n)
    s0 = vpack(s0)
    tl = _tile(length, 32, 1)
    y_t, s_last = _rwkv_scan(kvecs, vvec, param_t(k_k), param_t(k_a), s0, tl, dup)
    y = vunpack(y_t).reshape(length, A_HEAD_DIM, bn, A_HEADS)
    y = jnp.transpose(y, (2, 0, 3, 1)).reshape(bn * length, A_WIDTH)
    s_last = vunpack(s_last).reshape(A_HEAD_DIM, A_HEAD_DIM, bn, A_HEADS)
    s_last = jnp.transpose(s_last, (2, 3, 1, 0))
    return y, s_last


def _retention_kernel(p_ref, cos_ref, sin_ref, qdec_ref, kdec_ref, intra_ref, cdec_ref, s0_ref,
                      y_ref, sout_ref, s_scr, *, nsub, c, chained):
    ci = pl.program_id(1)
    rows = nsub * c

    if chained:
        @pl.when(ci == 0)
        def _():
            s_scr[...] = s0_ref[...]

    lane = lax.broadcasted_iota(jnp.int32, (rows, 2 * B_QK_WIDTH), 1)
    half = B_QK_DIM // 2
    qk = p_ref[:, 0:2 * B_QK_WIDTH]
    swapped = jnp.where((lane % B_QK_DIM) < half, pltpu.roll(qk, 2 * B_QK_WIDTH - half, 1),
                        pltpu.roll(qk, half, 1))
    rot = qk * cos_ref[...] + swapped * sin_ref[...]
    q = rot[:, :B_QK_WIDTH] * (B_QK_DIM ** -0.5)
    k = rot[:, B_QK_WIDTH:]
    qd = (q * qdec_ref[...]).astype(BF16)
    kd = (k * kdec_ref[...]).astype(BF16)
    qb = q.astype(BF16)
    kb = k.astype(BF16)
    for h in range(B_HEADS):
        qs = slice(h * B_QK_DIM, (h + 1) * B_QK_DIM)
        vs = slice(2 * B_QK_WIDTH + h * B_V_DIM, 2 * B_QK_WIDTH + (h + 1) * B_V_DIM)
        gs = slice(2 * B_QK_WIDTH + B_V_WIDTH + h * B_V_DIM,
                   2 * B_QK_WIDTH + B_V_WIDTH + (h + 1) * B_V_DIM)
        if chained:
            st = s_scr[0, h]
        for u in range(nsub):
            rs = slice(u * c, (u + 1) * c)
            if not chained:
                st = s0_ref[u, h]
            vh = p_ref[rs, vs].astype(BF16)
            gh = p_ref[rs, gs]
            scores = lax.dot_general(qb[rs, qs], kb[rs, qs], (((1,), (1,)), ((), ())),
                                     preferred_element_type=F32) * intra_ref[h]
            o = _dot(scores.astype(BF16), vh) + _dot(qd[rs, qs], st.astype(BF16))
            st = st * cdec_ref[h] + lax.dot_general(
                kd[rs, qs], vh, (((0,), (0,)), ((), ())), preferred_element_type=F32)
            o = o * lax.rsqrt(jnp.mean(o * o, -1, keepdims=True) + NORM_EPS)
            y_ref[rs, h * B_V_DIM:(h + 1) * B_V_DIM] = o * (gh * _sigmoid(gh))
            if not chained:
                sout_ref[u, h] = st
        if chained:
            s_scr[0, h] = st

    if chained:
        @pl.when(ci == pl.num_programs(1) - 1)
        def _():
            sout_ref[...] = s_scr[...]


def _retention(pb, s0, bn, length, pos0):
    c = math.gcd(length, RET_CHUNK)
    nc = length // c
    chained = nc > 1
    nsub = _tile(nc, 8, 1) if chained else _tile(bn, 8, 1)
    sb = 1 if chained else nsub
    nstep = nc // nsub if chained else 1
    half = B_QK_DIM // 2
    inv = ROPE_BASE ** (-jnp.arange(half, dtype=F32) / half)
    pos = (pos0 + jnp.arange(length)).astype(F32)
    ang = pos[:, None] * inv[None, :]
    cos, sin = jnp.cos(ang), jnp.sin(ang)
    cosf = jnp.tile(jnp.concatenate([cos, cos], -1), (1, 2 * B_HEADS))
    sinf = jnp.tile(jnp.concatenate([-sin, sin], -1), (1, 2 * B_HEADS))
    log_g = jnp.log1p(-jnp.exp2(-5.0 - jnp.arange(B_HEADS, dtype=F32)))
    idx = jnp.arange(c, dtype=F32)
    diff = idx[:, None] - idx[None, :]
    intra = jnp.where(diff >= 0, jnp.exp(jnp.maximum(diff, 0.0) * log_g[:, None, None]), 0.0)
    q_dec = jnp.exp((idx + 1.0)[:, None] * log_g[None, :])
    k_dec = jnp.exp((c - 1.0 - idx)[:, None] * log_g[None, :])
    c_dec = jnp.exp(c * log_g)
    rows = nsub * c
    qdec = jnp.tile(jnp.repeat(q_dec, B_QK_DIM, axis=1), (nsub, 1))
    kdec = jnp.tile(jnp.repeat(k_dec, B_QK_DIM, axis=1), (nsub, 1))
    cdec = jnp.broadcast_to(c_dec[:, None, None], (B_HEADS, 1, B_V_DIM))
    if not chained:
        cosf = jnp.tile(cosf, (nsub, 1))
        sinf = jnp.tile(sinf, (nsub, 1))

    full = lambda shape: pl.BlockSpec(shape, lambda i, j: (0,) * len(shape))
    tspec = pl.BlockSpec((rows, 2 * B_QK_WIDTH), lambda i, j: (j, 0))
    sspec = pl.BlockSpec((sb, B_HEADS, B_QK_DIM, B_V_DIM), lambda i, j: (i, 0, 0, 0))
    est = 2 * (rows * B_COLS * 4 + rows * B_V_WIDTH * 4 + 2 * sb * B_HEADS * B_QK_DIM * B_V_DIM * 4
               + 2 * rows * 2 * B_QK_WIDTH * 4) \
        + 3 * sb * B_HEADS * B_QK_DIM * B_V_DIM * 4 + 8 * rows * B_COLS * 4
    return pl.pallas_call(
        functools.partial(_retention_kernel, nsub=nsub, c=c, chained=chained),
        grid=(bn // sb, nstep),
        in_specs=[pl.BlockSpec((rows, B_COLS), lambda i, j: (i * nstep + j, 0)),
                  tspec, tspec,
                  full((rows, B_QK_WIDTH)), full((rows, B_QK_WIDTH)), full((B_HEADS, c, c)),
                  full((B_HEADS, 1, B_V_DIM)), sspec],
        out_specs=[pl.BlockSpec((rows, B_V_WIDTH), lambda i, j: (i * nstep + j, 0)), sspec],
        out_shape=[jax.ShapeDtypeStruct((bn * length, B_V_WIDTH), F32),
                   jax.ShapeDtypeStruct((bn, B_HEADS, B_QK_DIM, B_V_DIM), F32)],
        scratch_shapes=[pltpu.VMEM((sb, B_HEADS, B_QK_DIM, B_V_DIM), F32)],
        compiler_params=_params(("arbitrary", "arbitrary"), est),
        name="retention",
    )(pb, cosf, sinf, qdec, kdec, intra, cdec, s0.astype(F32))


def _mix_out_kernel(x_ref, y_ref, r_ref, k_ref, a_ref, v_ref, g_ref, yb_ref, ka_ref, rk_ref, lg_ref,
                    lb_ref, ones_ref, w_ref, o_ref):
    ones = ones_ref[...]
    y = y_ref[...]
    inv_d = 1.0 / A_HEAD_DIM
    mean = _seg_sum(y, ones) * inv_d
    d = y - mean
    var = _seg_sum(d * d, ones) * inv_d
    yn = d * lax.rsqrt(var + A_GN_EPS) * lg_ref[...] + lb_ref[...]
    v = v_ref[...]
    kmod = k_ref[...] * (1.0 + (a_ref[...] - 1.0) * ka_ref[...])
    bonus = _seg_sum(r_ref[...] * kmod * rk_ref[...], ones) * v
    ya = ((yn + bonus) * g_ref[...]).astype(BF16)
    o_ref[...] = (x_ref[...] + _dot(ya, w_ref[0:A_WIDTH, :])
                  + _dot(yb_ref[...].astype(BF16), w_ref[A_WIDTH:, :]))


def _mix_out(x, y, r, k, a, v, g, yb, k_a, r_k, lnx_g, lnx_b, ones, w_out):
    t, d = x.shape
    tm = _tile(t, 512)
    row = lambda n: pl.BlockSpec((tm, n), lambda i: (i, 0))
    full = lambda shape: pl.BlockSpec(shape, lambda i: (0,) * len(shape))
    est = 2 * (2 * tm * d * 4 + 7 * tm * A_WIDTH * 4 + (A_WIDTH + B_V_WIDTH) * d * 2) + 10 * tm * A_WIDTH * 4
    vec = lambda p: p.reshape(1, A_WIDTH)
    return pl.pallas_call(
        _mix_out_kernel,
        grid=(t // tm,),
        in_specs=[row(d)] + [row(A_WIDTH)] * 7 + [full((1, A_WIDTH))] * 4
                 + [full((A_WIDTH, A_WIDTH)), full((A_WIDTH + B_V_WIDTH, d))],
        out_specs=row(d),
        out_shape=jax.ShapeDtypeStruct((t, d), F32),
        compiler_params=_params(("arbitrary",), est),
        name="mix_out",
    )(x, y, r, k, a, v, g, yb, vec(k_a), vec(r_k), vec(lnx_g), vec(lnx_b), ones, w_out)


def _ffn_kernel(x_ref, g_ref, wg_ref, wu_ref, wd_ref, o_ref, h_scr):
    j = pl.program_id(1)

    @pl.when(j == 0)
    def _():
        x = x_ref[...]
        h_scr[...] = _rms(x, g_ref[...]).astype(BF16)
        o_ref[...] = x

    h = h_scr[...]
    a = _dot(h, wg_ref[...])
    b = _dot(h, wu_ref[...])
    m = (a * _sigmoid(a) * b).astype(BF16)
    o_ref[...] += _dot(m, wd_ref[...])


def _ffn(x, g, wg, wu, wd):
    t, d = x.shape
    ff = wg.shape[1]
    tm = _tile(t, 512)
    tf = _tile(ff, 1536, LANES)
    est = 2 * (2 * tm * d * 4 + 3 * d * tf * 2) + tm * d * 2 + 3 * tm * tf * 4
    return pl.pallas_call(
        _ffn_kernel,
        grid=(t // tm, ff // tf),
        in_specs=[pl.BlockSpec((tm, d), lambda i, j: (i, 0)),
                  pl.BlockSpec((1, d), lambda i, j: (0, 0)),
                  pl.BlockSpec((d, tf), lambda i, j: (0, j)),
                  pl.BlockSpec((d, tf), lambda i, j: (0, j)),
                  pl.BlockSpec((tf, d), lambda i, j: (j, 0))],
        out_specs=pl.BlockSpec((tm, d), lambda i, j: (i, 0)),
        out_shape=jax.ShapeDtypeStruct((t, d), F32),
        scratch_shapes=[pltpu.VMEM((tm, d), BF16)],
        compiler_params=_params(("arbitrary", "arbitrary"), est),
        name="ffn",
    )(x, g.reshape(1, d), wg, wu, wd)


def _rglru_kernel(gate_ref, xb_ref, tail0_ref, h0_ref, cw_ref, cb_ref, wa_ref, ba_ref, wx_ref,
                  bx_ref, lam_ref, y_ref, hlast_ref, tail_scr, h_scr, a_scr, b_scr, hs_scr, *, tl):
    tb = pl.program_id(1)

    @pl.when(tb == 0)
    def _():
        tail_scr[...] = tail0_ref[0]
        h_scr[...] = h0_ref[0]

    xb = xb_ref[...]
    full = jnp.concatenate([tail_scr[...], xb], axis=0)
    xc = cb_ref[...]
    for j in range(CONV_W):
        shift = CONV_W - 1 - j
        term = pltpu.roll(full, shift, 0) if shift else full
        xc = xc + term[SUBLANES:, :] * cw_ref[j:j + 1, :]
    tail_scr[...] = xb[tl - SUBLANES:, :]
    xcb = xc.astype(BF16)
    for n in range(C_BLOCKS):
        sl = slice(n * C_BLOCK_DIM, (n + 1) * C_BLOCK_DIM)
        xn = xcb[:, sl]
        r = _sigmoid(_dot(xn, wa_ref[n]) + ba_ref[:, sl])
        i = _sigmoid(_dot(xn, wx_ref[n]) + bx_ref[:, sl])
        lam = lam_ref[:, sl]
        softplus_neg_lam = jnp.maximum(-lam, 0.0) + jnp.log1p(jnp.exp(-jnp.abs(lam)))
        log_a = -LRU_C * r * softplus_neg_lam
        a = jnp.exp(log_a)
        gain = jnp.sqrt(-jnp.tanh(log_a) * (a * a + 1.0))
        a_scr[:, sl] = a
        b_scr[:, sl] = gain * i * xc[:, sl]

    def row(t, h):
        h = a_scr[pl.ds(t, 1), :] * h + b_scr[pl.ds(t, 1), :]
        hs_scr[pl.ds(t, 1), :] = h
        return h

    h = lax.fori_loop(0, tl, row, h_scr[...], unroll=8)
    h_scr[...] = h
    hlast_ref[0] = h
    gate = gate_ref[...]
    cdf = 0.5 * (1.0 + jnp.tanh(math.sqrt(2.0 / math.pi) * (gate + 0.044715 * (gate * gate * gate))))
    y_ref[...] = (gate * cdf * hs_scr[...]).astype(BF16)


def _rglru(gate, xb, tail0, h0, wts, bn, length):
    tl = _tile(length, 256)
    nt = length // tl
    full = lambda shape: pl.BlockSpec(shape, lambda i, j: (0,) * len(shape))
    row = pl.BlockSpec((tl, D_RNN), lambda i, j: (i * nt + j, 0))
    est = 2 * (3 * tl * D_RNN * 4 + 2 * C_BLOCKS * C_BLOCK_DIM * C_BLOCK_DIM * 2) + 10 * tl * D_RNN * 4
    return pl.pallas_call(
        functools.partial(_rglru_kernel, tl=tl),
        grid=(bn, nt),
        in_specs=[row, row,
                  pl.BlockSpec((1, SUBLANES, D_RNN), lambda i, j: (i, 0, 0)),
                  pl.BlockSpec((1, 1, D_RNN), lambda i, j: (i, 0, 0)),
                  full((CONV_W, D_RNN)), full((1, D_RNN)),
                  full((C_BLOCKS, C_BLOCK_DIM, C_BLOCK_DIM)), full((1, D_RNN)),
                  full((C_BLOCKS, C_BLOCK_DIM, C_BLOCK_DIM)), full((1, D_RNN)),
                  full((1, D_RNN))],
        out_specs=[row, pl.BlockSpec((1, 1, D_RNN), lambda i, j: (i, 0, 0))],
        out_shape=[jax.ShapeDtypeStruct((bn * length, D_RNN), BF16),
                   jax.ShapeDtypeStruct((bn, 1, D_RNN), F32)],
        scratch_shapes=[pltpu.VMEM((SUBLANES, D_RNN), F32), pltpu.VMEM((1, D_RNN), F32),
                        pltpu.VMEM((tl, D_RNN), F32), pltpu.VMEM((tl, D_RNN), F32),
                        pltpu.VMEM((tl, D_RNN), F32)],
        compiler_params=_params(("arbitrary", "arbitrary"), est),
        name="rglru",
    )(gate, xb, tail0, h0, *wts)


def _matmul_res_kernel(y_ref, w_ref, x_ref, o_ref):
    o_ref[...] = x_ref[...] + _dot(y_ref[...], w_ref[...])


def _matmul_res(y, w, x):
    t, d = x.shape
    kdim = y.shape[1]
    tm = _tile(t, 512, 16)
    est = 2 * (tm * kdim * 2 + kdim * d * 2 + 2 * tm * d * 4)
    return pl.pallas_call(
        _matmul_res_kernel,
        grid=(t // tm,),
        in_specs=[pl.BlockSpec((tm, kdim), lambda i: (i, 0)),
                  pl.BlockSpec((kdim, d), lambda i: (0, 0)),
                  pl.BlockSpec((tm, d), lambda i: (i, 0))],
        out_specs=pl.BlockSpec((tm, d), lambda i: (i, 0)),
        out_shape=jax.ShapeDtypeStruct((t, d), F32),
        compiler_params=_params(("arbitrary",), est),
        name="matmul_res",
    )(y, w, x)


def _router_kernel(x_ref, g_ref, wr_ref, h_ref, route_ref, *, tm):
    h = _rms(x_ref[...], g_ref[...])
    h_ref[...] = h
    logits = jnp.dot(h, wr_ref[...], preferred_element_type=F32, precision=lax.Precision.HIGHEST)
    lane = lax.broadcasted_iota(jnp.int32, (tm, LANES), 1).astype(F32)
    neg = jnp.float32(-jnp.inf)
    lg = jnp.where(lane < N_EXPERTS, logits, neg)
    m1 = jnp.max(lg, -1, keepdims=True)
    i1 = jnp.min(jnp.where(lg == m1, lane, float(LANES)), -1, keepdims=True)
    lg2 = jnp.where(lane == i1, neg, lg)
    m2 = jnp.max(lg2, -1, keepdims=True)
    i2 = jnp.min(jnp.where(lg2 == m2, lane, float(LANES)), -1, keepdims=True)
    e = jnp.exp(m2 - m1)
    g1 = 1.0 / (1.0 + e)
    g2 = e / (1.0 + e)
    route_ref[...] = jnp.where(lane == 0, i1, jnp.where(lane == 1, i2, jnp.where(
        lane == 2, g1, jnp.where(lane == 3, g2, 0.0))))


def _router(x, g, wr_pad):
    t, d = x.shape
    tm = _tile(t, 512)
    est = 2 * (2 * tm * d * 4 + d * LANES * 4 + tm * LANES * 4) + 2 * tm * d * 4
    return pl.pallas_call(
        functools.partial(_router_kernel, tm=tm),
        grid=(t // tm,),
        in_specs=[pl.BlockSpec((tm, d), lambda i: (i, 0)),
                  pl.BlockSpec((1, d), lambda i: (0, 0)),
                  pl.BlockSpec((d, LANES), lambda i: (0, 0))],
        out_specs=[pl.BlockSpec((tm, d), lambda i: (i, 0)),
                   pl.BlockSpec((tm, LANES), lambda i: (i, 0))],
        out_shape=[jax.ShapeDtypeStruct((t, d), F32), jax.ShapeDtypeStruct((t, LANES), F32)],
        compiler_params=_params(("arbitrary",), est),
        name="router",
    )(x, g.reshape(1, d), wr_pad)


def _moe_kernel(be_ref, bv_ref, tok_hbm, h_hbm, wg_ref, wu_ref, wd_ref, o_ref,
                idx_smem, xbuf, xbf, sem_idx, sem_rows, *, tm):
    i = pl.program_id(0)
    j = pl.program_id(1)
    nb = pl.num_programs(0)
    valid = bv_ref[i] != 0
    slot = i % 2
    nxt = jnp.minimum(i + 1, nb - 1)

    def idx_copy(blk, s):
        return pltpu.make_async_copy(tok_hbm.at[pl.ds(blk * tm, tm)], idx_smem.at[s], sem_idx.at[s])

    def row_copy(tok, s, r):
        return pltpu.make_async_copy(h_hbm.at[pl.ds(tok, 1)], xbuf.at[s, pl.ds(r, 1)],
                                     sem_rows.at[s])

    def issue_rows(s):
        def issue(r, c):
            row_copy(idx_smem[s, r], s, r).start()
            return c
        lax.fori_loop(0, tm, issue, 0, unroll=8)

    def drain_rows(s):
        def drain(r, c):
            row_copy(0, s, r).wait()
            return c
        lax.fori_loop(0, tm, drain, 0, unroll=8)

    @pl.when(jnp.logical_and(valid, j == 0))
    def _():
        @pl.when(i == 0)
        def _():
            first = idx_copy(0, 0)
            first.start()
            first.wait()
            issue_rows(0)

        ahead = idx_copy(nxt, 1 - slot)
        ahead.start()
        drain_rows(slot)
        xbf[...] = xbuf[slot].astype(BF16)
        ahead.wait()
        issue_rows(1 - slot)

    @pl.when(valid)
    def _():
        x = xbf[...]
        a = _dot(x, wg_ref[...])
        b = _dot(x, wu_ref[...])
        m = (a * _sigmoid(a) * b).astype(BF16)
        contrib = _dot(m, wd_ref[...])

        @pl.when(j == 0)
        def _():
            o_ref[...] = contrib

        @pl.when(j > 0)
        def _():
            o_ref[...] += contrib

    is_last = jnp.logical_or(i == nb - 1, bv_ref[nxt] == 0)

    @pl.when(jnp.logical_and(jnp.logical_and(valid, is_last), j == pl.num_programs(1) - 1))
    def _():
        drain_rows(1 - slot)

    @pl.when(jnp.logical_and(jnp.logical_not(valid), j == 0))
    def _():
        o_ref[...] = jnp.zeros_like(o_ref)


def _moe(block_e, block_valid, tok_sorted, h, wg, wu, wd, tm):
    p = tok_sorted.shape[0]
    d = h.shape[1]
    ff = wg.shape[2]
    tf = _tile(ff, 1792, LANES)
    nf = ff // tf
    nb = p // tm

    def wcol(i, j, be, bv):
        return (be[i], 0, jnp.where(bv[i] != 0, j, nf - 1))

    def wrow(i, j, be, bv):
        return (be[i], jnp.where(bv[i] != 0, j, nf - 1), 0)

    est = 2 * (3 * d * tf * 2 + tm * d * 4) + 2 * tm * d * 4 + tm * d * 2 + 3 * tm * tf * 4
    grid_spec = pltpu.PrefetchScalarGridSpec(
        num_scalar_prefetch=2,
        grid=(nb, nf),
        in_specs=[pl.BlockSpec(memory_space=pl.ANY),
                  pl.BlockSpec(memory_space=pl.ANY),
                  pl.BlockSpec((None, d, tf), wcol),
                  pl.BlockSpec((None, d, tf), wcol),
                  pl.BlockSpec((None, tf, d), wrow)],
        out_specs=pl.BlockSpec((tm, d), lambda i, j, be, bv: (i, 0)),
        scratch_shapes=[pltpu.SMEM((2, tm), jnp.int32), pltpu.VMEM((2, tm, d), F32),
                        pltpu.VMEM((tm, d), BF16), pltpu.SemaphoreType.DMA((2,)),
                        pltpu.SemaphoreType.DMA((2,))],
    )
    return pl.pallas_call(
        functools.partial(_moe_kernel, tm=tm),
        grid_spec=grid_spec,
        out_shape=jax.ShapeDtypeStruct((p, d), F32),
        compiler_params=_params(("arbitrary", "arbitrary"), est),
        name="moe",
    )(block_e, block_valid, tok_sorted, h, wg, wu, wd)


def _combine_kernel(pos_hbm, y_hbm, x_ref, route_ref, g_ref, o_ref, idx_smem, ybuf, sem_idx,
                    sem_rows, *, tm):
    i = pl.program_id(0)
    slot = i % 2
    nrows = 2 * tm

    def row_copy(src, s, r):
        return pltpu.make_async_copy(y_hbm.at[pl.ds(src, 1)], ybuf.at[s, pl.ds(r, 1)],
                                     sem_rows.at[s])

    def fetch(tile, s):
        idx_copy = pltpu.make_async_copy(pos_hbm.at[pl.ds(tile * nrows, nrows)], idx_smem.at[s],
                                         sem_idx.at[s])
        idx_copy.start()
        idx_copy.wait()

        def issue(r, c):
            row_copy(idx_smem[s, r], s, r).start()
            return c

        lax.fori_loop(0, nrows, issue, 0, unroll=8)

    @pl.when(i == 0)
    def _():
        fetch(0, 0)

    @pl.when(i + 1 < pl.num_programs(0))
    def _():
        fetch(i + 1, 1 - slot)

    def drain(r, c):
        row_copy(0, slot, r).wait()
        return c

    lax.fori_loop(0, nrows, drain, 0, unroll=8)
    route = route_ref[...]
    g1 = route[:, TOP_K:TOP_K + 1]
    g2 = route[:, TOP_K + 1:TOP_K + 2]
    x = x_ref[...] + g1 * ybuf[slot, 0:tm, :] + g2 * ybuf[slot, tm:nrows, :]
    o_ref[...] = _rms(x, g_ref[...])


def _combine(pos, y_sorted, x, route, g, tm):
    t, d = x.shape
    est = 2 * (2 * tm * d * 4 + tm * LANES * 4) + 4 * tm * d * 4 + 2 * tm * d * 4
    return pl.pallas_call(
        functools.partial(_combine_kernel, tm=tm),
        grid=(t // tm,),
        in_specs=[pl.BlockSpec(memory_space=pl.ANY),
                  pl.BlockSpec(memory_space=pl.ANY),
                  pl.BlockSpec((tm, d), lambda i: (i, 0)),
                  pl.BlockSpec((tm, LANES), lambda i: (i, 0)),
                  pl.BlockSpec((1, d), lambda i: (0, 0))],
        out_specs=pl.BlockSpec((tm, d), lambda i: (i, 0)),
        out_shape=jax.ShapeDtypeStruct((t, d), F32),
        scratch_shapes=[pltpu.SMEM((2, 2 * tm), jnp.int32), pltpu.VMEM((2, 2 * tm, d), F32),
                        pltpu.SemaphoreType.DMA((2,)), pltpu.SemaphoreType.DMA((2,))],
        compiler_params=_params(("arbitrary",), est),
        name="combine",
    )(pos, y_sorted, x, route, g.reshape(1, d))


def _route_plan(route, tm):
    t = route.shape[0]
    flat_e = route[:, 0:TOP_K].astype(jnp.int32).reshape(-1)
    flat_tok = jnp.repeat(jnp.arange(t, dtype=jnp.int32), TOP_K)
    onehot = (flat_e[:, None] == jnp.arange(N_EXPERTS, dtype=jnp.int32)[None, :]).astype(jnp.int32)
    csum = jnp.cumsum(onehot, axis=0)
    counts = csum[-1]
    rank = jnp.sum((csum - onehot) * onehot, axis=1)
    padded = ((counts + tm - 1) // tm) * tm
    pend = jnp.cumsum(padded)
    pstart = pend - padded
    dest = pstart[flat_e] + rank
    nb = (t * TOP_K + tm - 1) // tm + N_EXPERTS
    p = nb * tm
    tok_sorted = jnp.zeros((p,), jnp.int32).at[dest].set(flat_tok)
    block_start = jnp.arange(nb, dtype=jnp.int32) * tm
    block_valid = (block_start < pend[-1]).astype(jnp.int32)
    last_e = jnp.sum((pend <= pend[-1] - 1).astype(jnp.int32))
    block_e = jnp.sum((pend[None, :] <= block_start[:, None]).astype(jnp.int32), axis=1)
    block_e = jnp.where(block_valid != 0, jnp.minimum(block_e, N_EXPERTS - 1), last_e)
    return tok_sorted, block_e.astype(jnp.int32), block_valid, dest.reshape(t, TOP_K)


def _tile_pos(dest, tm):
    t = dest.shape[0]
    return jnp.transpose(dest.reshape(t // tm, tm, TOP_K), (0, 2, 1)).reshape(-1)


def _prep_weights(w):
    bf = lambda a: a.astype(BF16)
    ones = (jnp.arange(A_WIDTH)[:, None] // A_HEAD_DIM == jnp.arange(A_WIDTH)[None, :] // A_HEAD_DIM)
    lora = jnp.zeros((A_LORA_COLS, A_WIDTH), F32)
    out = dict(w)
    out['ones'] = ones.astype(BF16)
    out['wdec_pad'] = bf(lora.at[0:64].set(w['a_w_decay'][0]))
    out['wiclr_pad'] = bf(lora.at[64:128].set(w['a_w_iclr'][0]))
    out['wgate_pad'] = bf(lora.at[128:256].set(w['a_w_gate'][0]))
    for name in ('w_in0', 'w_out0', 'ffn_gate', 'ffn_up', 'ffn_down', 'w_in1', 'w_out1', 'c_w_a',
                 'c_w_x', 'moe_gate', 'moe_up', 'moe_down'):
        out[name] = bf(w[name][0])
    out['router_pad'] = jnp.zeros((w['moe_router'].shape[1], LANES), F32).at[:, :N_EXPERTS].set(
        w['moe_router'][0])
    return out


def _layer0(x, bn, length, pos0, shift, wkv, ret, w):
    t = bn * length
    pa, pb = _norm_matmul(x, w['norm_mix0'][0], w['w_in0'], (A_COLS, B_COLS))
    tm = _tile(t, 256) if length >= 256 else _tile(t, 256, length)
    if length >= tm:
        starts = jnp.arange(t // tm) * tm
        before = pa[jnp.maximum(starts - 1, 0)]
        first = jnp.where((starts % length == 0)[:, None], shift.astype(F32)[starts // length], before)
        ext = jnp.zeros((t // tm, SUBLANES, A_COLS), F32).at[:, 0].set(first).reshape(-1, A_COLS)
    else:
        ext = jnp.repeat(shift.astype(F32), length, axis=0)
    row = lambda a: a[0].reshape(1, -1)
    r, dec, k, a, v, g = _rwkv_prep(
        pa, ext, length, tm,
        (row(w['a_mu']), row(w['a_w0']), w['wdec_pad'], row(w['a_a0']), w['wiclr_pad'],
         w['wgate_pad']))
    y, wkv_new = _rwkv_recurrence(r, dec, k, a, v, w['a_k_k'][0], w['a_k_a'][0], wkv, bn, length)
    yb, ret_new = _retention(pb, ret, bn, length, pos0)
    x = _mix_out(x, y, r, k, a, v, g, yb, w['a_k_a'][0], w['a_r_k'][0].reshape(-1), w['a_lnx_g'][0],
                 w['a_lnx_b'][0], w['ones'], w['w_out0'])
    x = _ffn(x, w['norm_ffn0'][0], w['ffn_gate'], w['ffn_up'], w['ffn_down'])
    shift_new = pa.reshape(bn, length, A_COLS)[:, -1]
    return x, shift_new, wkv_new, ret_new


def _layer1_mixer(x, bn, length, conv, hlru, w):
    assert length >= CONV_W - 1
    gate, xb = _norm_matmul(x, w['norm_mix1'][0], w['w_in1'], (D_RNN, D_RNN))
    tail0 = jnp.zeros((bn, SUBLANES, D_RNN), F32).at[:, SUBLANES - (CONV_W - 1):].set(conv.astype(F32))
    row = lambda a: a[0].reshape(1, -1)
    y, h_last = _rglru(gate, xb, tail0, hlru.astype(F32).reshape(bn, 1, D_RNN),
                       (w['c_conv_w'][0], row(w['c_conv_b']), w['c_w_a'], row(w['c_b_a']),
                        w['c_w_x'], row(w['c_b_x']), row(w['c_lambda'])), bn, length)
    x = _matmul_res(y, w['w_out1'], x)
    conv_new = xb.reshape(bn, length, D_RNN)[:, length - (CONV_W - 1):]
    return x, conv_new, h_last.reshape(bn, D_RNN)


def kernel(x_prompt, x_sample, state_rwkv_shift, state_rwkv_wkv, state_ret, state_lru_conv, state_lru_h, norm_mix0, w_in0, a_mu, a_w0, a_w_decay, a_a0, a_w_iclr, a_w_gate, a_k_k, a_k_a, a_r_k, a_lnx_g, a_lnx_b, w_out0, norm_ffn0, ffn_gate, ffn_up, ffn_down, norm_mix1, w_in1, c_conv_w, c_conv_b, c_w_a, c_b_a, c_w_x, c_b_x, c_lambda, w_out1, norm_ffn1, moe_router, moe_gate, moe_up, moe_down, norm_final):
    w = _prep_weights(dict(
        norm_mix0=norm_mix0, w_in0=w_in0, a_mu=a_mu, a_w0=a_w0, a_w_decay=a_w_decay, a_a0=a_a0,
        a_w_iclr=a_w_iclr, a_w_gate=a_w_gate, a_k_k=a_k_k, a_k_a=a_k_a, a_r_k=a_r_k, a_lnx_g=a_lnx_g,
        a_lnx_b=a_lnx_b, w_out0=w_out0, norm_ffn0=norm_ffn0, ffn_gate=ffn_gate, ffn_up=ffn_up,
        ffn_down=ffn_down, norm_mix1=norm_mix1, w_in1=w_in1, c_conv_w=c_conv_w, c_conv_b=c_conv_b,
        c_w_a=c_w_a, c_b_a=c_b_a, c_w_x=c_w_x, c_b_x=c_b_x, c_lambda=c_lambda, w_out1=w_out1,
        norm_ffn1=norm_ffn1, moe_router=moe_router, moe_gate=moe_gate, moe_up=moe_up,
        moe_down=moe_down))
    dt = x_prompt.dtype
    d = x_prompt.shape[-1]
    bp, lp = x_prompt.shape[:2]
    bs, ls = x_sample.shape[:2]
    zeros = lambda shape: jnp.zeros(shape, F32)
    groups = [
        (x_prompt.reshape(bp * lp, d), bp, lp, 0, zeros((bp, A_COLS)),
         zeros((bp, A_HEADS, A_HEAD_DIM, A_HEAD_DIM)), zeros((bp, B_HEADS, B_QK_DIM, B_V_DIM)),
         zeros((bp, CONV_W - 1, D_RNN)), zeros((bp, D_RNN))),
        (x_sample.reshape(bs * ls, d), bs, ls, PAST_LEN, state_rwkv_shift[0], state_rwkv_wkv[0],
         state_ret[0], state_lru_conv[0], state_lru_h[0]),
    ]
    xs, states, hs, routes = [], [], [], []
    for x, bn, length, pos0, shift, wkv, ret, conv, hlru in groups:
        x, s_shift, s_wkv, s_ret = _layer0(x, bn, length, pos0, shift, wkv, ret, w)
        x, s_conv, s_h = _layer1_mixer(x, bn, length, conv, hlru, w)
        h, route = _router(x, w['norm_ffn1'][0], w['router_pad'])
        xs.append(x)
        hs.append(h)
        routes.append(route)
        states.append((s_shift, s_wkv, s_ret, s_conv, s_h))
    h_all = jnp.concatenate(hs, 0)
    route_all = jnp.concatenate(routes, 0)
    tm_moe = 512 if h_all.shape[0] >= 4096 else 64
    tok_sorted, block_e, block_valid, dest = _route_plan(route_all, tm_moe)
    y_sorted = _moe(block_e, block_valid, tok_sorted, h_all, w['moe_gate'], w['moe_up'],
                    w['moe_down'], tm_moe)
    outs = []
    off = 0
    for x, route in zip(xs, routes):
        t = x.shape[0]
        tm = _tile(t, 256)
        pos = _tile_pos(dest[off:off + t], tm)
        outs.append(_combine(pos, y_sorted, x, route, norm_final, tm))
        off += t
    y_prompt = outs[0].reshape(bp, lp, d)
    y_sample = outs[1].reshape(bs, ls, d)
    st_p = tuple(s[None].astype(dt) for s in states[0])
    st_s = tuple(s[None].astype(dt) for s in states[1])
    return (y_prompt, y_sample) + st_p + st_s
```

```python
import functools
import math

import jax
import jax.numpy as jnp
import numpy as np
from jax import lax
from jax.experimental import pallas as pl
from jax.experimental.pallas import tpu as pltpu

F32 = jnp.float32
BF16 = jnp.bfloat16

A_HEADS = 8
A_HEAD_DIM = 64
A_WIDTH = A_HEADS * A_HEAD_DIM
A_LORA_COLS = 256
A_COLS = 3 * A_WIDTH + A_LORA_COLS
A_GN_EPS = A_HEAD_DIM * 1e-5
B_HEADS = 4
B_QK_DIM = 64
B_V_DIM = 128
B_QK_WIDTH = B_HEADS * B_QK_DIM
B_V_WIDTH = B_HEADS * B_V_DIM
B_COLS = 2 * B_QK_WIDTH + 2 * B_V_WIDTH
RET_CHUNK = 64
ROPE_BASE = 10000.0
D_RNN = 1280
C_BLOCKS = 10
C_BLOCK_DIM = D_RNN // C_BLOCKS
CONV_W = 4
LRU_C = 8.0
N_EXPERTS = 8
TOP_K = 2
NORM_EPS = 1e-6
PAST_LEN = 16384

LANES = 128
SUBLANES = 8
VMEM_PHYSICAL_BYTES = 64 * 1024 * 1024
VMEM_BUDGET_BYTES = VMEM_PHYSICAL_BYTES - 4 * 1024 * 1024


def _tile(n, pref, mult=SUBLANES):
    t = min(pref, n)
    while t > mult and (n % t or t % mult):
        t -= 1
    assert n % t == 0 and t % mult == 0, (n, pref, mult)
    return t


def _params(sem, est_bytes):
    limit = int(min(max(est_bytes * 5 // 4 + (4 << 20), 32 << 20), VMEM_BUDGET_BYTES))
    return pltpu.CompilerParams(dimension_semantics=sem, vmem_limit_bytes=limit)


def _rms(x, g):
    return x * lax.rsqrt(jnp.mean(x * x, -1, keepdims=True) + NORM_EPS) * g


def _dot(a, b):
    return jnp.dot(a, b, preferred_element_type=F32)


def _sigmoid(x):
    return 1.0 / (1.0 + jnp.exp(-x))


def _seg_sum(x, ones):
    hi = x.astype(BF16)
    lo = (x - hi.astype(F32)).astype(BF16)
    return _dot(hi, ones) + _dot(lo, ones)


def _norm_matmul_kernel(x_ref, g_ref, w_ref, *o_refs, splits):
    h = _rms(x_ref[...], g_ref[...]).astype(BF16)
    off = 0
    for o_ref, n in zip(o_refs, splits):
        o_ref[...] = _dot(h, w_ref[:, off:off + n])
        off += n


def _norm_matmul(x, g, w, splits):
    t, d = x.shape
    n = w.shape[1]
    tm = _tile(t, 512)
    est = 2 * (tm * d * 4 + d * n * 2 + tm * n * 4) + tm * n * 4
    return pl.pallas_call(
        functools.partial(_norm_matmul_kernel, splits=splits),
        grid=(t // tm,),
        in_specs=[pl.BlockSpec((tm, d), lambda i: (i, 0)),
                  pl.BlockSpec((1, d), lambda i: (0, 0)),
                  pl.BlockSpec((d, n), lambda i: (0, 0))],
        out_specs=[pl.BlockSpec((tm, s), lambda i: (i, 0)) for s in splits],
        out_shape=[jax.ShapeDtypeStruct((t, s), F32) for s in splits],
        compiler_params=_params(("arbitrary",), est),
        name="norm_matmul",
    )(x, g.reshape(1, d), w)


def _rwkv_prep_kernel(p_ref, ext_ref, mu_ref, w0_ref, wdec_ref, a0_ref, wiclr_ref, wgate_ref,
                      r_o, w_o, k_o, a_o, v_o, g_o, *cm_outs, seq_len, tm):
    p = p_ref[...]
    rolled = pltpu.roll(p, 1, 0)
    row = lax.broadcasted_iota(jnp.int32, (tm, 1), 0)
    if seq_len >= tm:
        prev = jnp.where(row == 0, ext_ref[0:1, :], rolled)
    else:
        prev = jnp.where(row % seq_len == 0, ext_ref[...], rolled)
    pm = p + (prev - p) * mu_ref[...]
    r = pm[:, 0:A_WIDTH]
    k = pm[:, A_WIDTH:2 * A_WIDTH]
    v = pm[:, 2 * A_WIDTH:3 * A_WIDTH]
    tail = pm[:, 3 * A_WIDTH:A_COLS]
    w_pre = w0_ref[...] + _dot(jnp.tanh(tail).astype(BF16), wdec_ref[...])
    w_log = jnp.minimum(w_pre, 0.0) - jnp.log1p(jnp.exp(-jnp.abs(w_pre))) - 0.5
    decay = jnp.exp(-jnp.exp(w_log))
    a = _sigmoid(a0_ref[...] + _dot(tail.astype(BF16), wiclr_ref[...]))
    g = _dot(_sigmoid(tail).astype(BF16), wgate_ref[...])
    r_o[...] = r
    w_o[...] = decay
    k_o[...] = k
    a_o[...] = a
    v_o[...] = v
    g_o[...] = g
    for o_ref, val in zip(cm_outs, (r, decay, k, a, v)):
        o_ref[...] = val.T


def _rwkv_prep(pa, ext, seq_len, tm, wts, channel_major):
    t = pa.shape[0]
    full = lambda shape: pl.BlockSpec(shape, lambda i: (0,) * len(shape))
    ext_rows = SUBLANES if seq_len >= tm else tm
    ncm = 5 if channel_major else 0
    est = 2 * (tm * A_COLS * 4 * 2 + (6 + ncm) * tm * A_WIDTH * 4) + 12 * tm * A_WIDTH * 4
    out_specs = [pl.BlockSpec((tm, A_WIDTH), lambda i: (i, 0))] * 6
    out_shape = [jax.ShapeDtypeStruct((t, A_WIDTH), F32)] * 6
    if channel_major:
        nt = seq_len // tm
        out_specs += [pl.BlockSpec((A_WIDTH, tm), lambda i: (i // nt, i % nt))] * ncm
        out_shape += [jax.ShapeDtypeStruct((t // seq_len * A_WIDTH, seq_len), F32)] * ncm
    return pl.pallas_call(
        functools.partial(_rwkv_prep_kernel, seq_len=seq_len, tm=tm),
        grid=(t // tm,),
        in_specs=[pl.BlockSpec((tm, A_COLS), lambda i: (i, 0)),
                  pl.BlockSpec((ext_rows, A_COLS), lambda i: (i, 0)),
                  full((1, A_COLS)), full((1, A_WIDTH)), full((A_LORA_COLS, A_WIDTH)),
                  full((1, A_WIDTH)), full((A_LORA_COLS, A_WIDTH)), full((A_LORA_COLS, A_WIDTH))],
        out_specs=out_specs,
        out_shape=out_shape,
        compiler_params=_params(("arbitrary",), est),
        name="rwkv_prep",
    )(pa, ext, *wts)


def _rwkv_scan_kernel(r_in, w_in, k_in, a_in, v_in, kkp_ref, kap_ref, s0_ref, y_out, s_scr,
                      r_ref, w_ref, k_ref, kk_ref, b_ref, *cm_scr, tl, nv, dup, cm):
    tb = pl.program_id(1)
    nvg = nv // SUBLANES
    kdim = A_HEAD_DIM
    n = LANES // dup

    @pl.when(tb == 0)
    def _():
        s_scr[...] = s0_ref[...]

    if cm:
        v_scr, y_scr = cm_scr

        nseq = n // A_HEADS
        seq_rows = kdim * A_HEADS

        def tload(ref, chans):
            def first_row(c):
                off = c * A_HEADS
                return off if isinstance(c, int) else pl.multiple_of(off, A_HEADS)

            tiles = [ref[pl.ds(b * seq_rows + first_row(c), A_HEADS), :]
                     for c in chans for b in range(nseq)]
            return jnp.concatenate(tiles, axis=0).T

        def fill(c, ss):
            rows = pl.ds(pl.multiple_of(c * tl, tl), tl)
            r_ref[rows, :] = tload(r_in, [c] * dup)
            w_ref[rows, :] = tload(w_in, [c] * dup)
            kc = tload(k_in, [c] * dup)
            ac = tload(a_in, [c] * dup)
            kkc = kc * kkp_ref[pl.ds(c, 1), :]
            k_ref[rows, :] = kc * (1.0 + (ac - 1.0) * kap_ref[pl.ds(c, 1), :])
            kk_ref[rows, :] = kkc
            b_ref[rows, :] = ac
            return ss + kkc * kkc

        ss = lax.fori_loop(0, kdim, fill, jnp.zeros((tl, LANES), F32))
        denom = jnp.maximum(jnp.sqrt(ss), 1e-12)

        def normalise(c, carry):
            rows = pl.ds(pl.multiple_of(c * tl, tl), tl)
            kkn = kk_ref[rows, :] / denom
            kk_ref[rows, :] = kkn
            b_ref[rows, :] = -(kkn * b_ref[rows, :])
            return carry

        lax.fori_loop(0, kdim, normalise, 0)
        for r in range(nv):
            v_scr[pl.ds(r, tl, stride=nv), :] = tload(v_in, [vh * nv + r for vh in range(dup)])

        def vtile(t, vg):
            return v_scr[pl.ds(pl.multiple_of(t * nv, SUBLANES) + vg * SUBLANES, SUBLANES), :]

        def ystore(t, vg, val):
            y_scr[pl.ds(pl.multiple_of(t * nv, SUBLANES) + vg * SUBLANES, SUBLANES), :] = val
    else:
        def lanes(x):
            return jnp.concatenate([x] * dup, axis=-1) if dup > 1 else x

        def flat(x):
            return x.reshape(tl * kdim, LANES)

        k = lanes(k_in[...])
        a = lanes(a_in[...])
        kk = k * kkp_ref[...]
        kkn = kk / jnp.maximum(jnp.sqrt(jnp.sum(kk * kk, axis=1, keepdims=True)), 1e-12)
        r_ref[...] = flat(lanes(r_in[...]))
        w_ref[...] = flat(lanes(w_in[...]))
        k_ref[...] = flat(k * (1.0 + (a - 1.0) * kap_ref[...]))
        kk_ref[...] = flat(kkn)
        b_ref[...] = flat(-(kkn * a))

        def vtile(t, vg):
            return v_in[t, vg * SUBLANES:(vg + 1) * SUBLANES, :]

        def ystore(t, vg, val):
            y_out[t, vg * SUBLANES:(vg + 1) * SUBLANES, :] = val

    def bcast(ref, t, k):
        row = k * tl + t if cm else t * kdim + k
        return jnp.broadcast_to(ref[pl.ds(row, 1), :], (SUBLANES, LANES))

    def step(t, carry):
        sa = [jnp.zeros((SUBLANES, LANES), F32) for _ in range(nvg)]
        for k in range(kdim):
            kk_row = bcast(kk_ref, t, k)
            for vg in range(nvg):
                sa[vg] = sa[vg] + s_scr[k, vg * SUBLANES:(vg + 1) * SUBLANES, :] * kk_row
        vv = [vtile(t, vg) for vg in range(nvg)]
        y = [jnp.zeros((SUBLANES, LANES), F32) for _ in range(nvg)]
        for k in range(kdim):
            w_row = bcast(w_ref, t, k)
            b_row = bcast(b_ref, t, k)
            k_row = bcast(k_ref, t, k)
            r_row = bcast(r_ref, t, k)
            for vg in range(nvg):
                rows = slice(vg * SUBLANES, (vg + 1) * SUBLANES)
                s_new = s_scr[k, rows, :] * w_row + sa[vg] * b_row + vv[vg] * k_row
                s_scr[k, rows, :] = s_new
                y[vg] = y[vg] + s_new * r_row
        for vg in range(nvg):
            ystore(t, vg, y[vg])
        return carry

    lax.fori_loop(0, tl, step, 0)

    if cm:
        for r in range(nv):
            yt = y_scr[pl.ds(r, tl, stride=nv), :].T
            for vh in range(dup):
                for b in range(nseq):
                    y_out[pl.ds(b * seq_rows + (vh * nv + r) * A_HEADS, A_HEADS), :] = (
                        yt[vh * n + b * A_HEADS:vh * n + (b + 1) * A_HEADS, :])


def _rwkv_scan(ops, kkp, kap, s0, length, tl, nv, dup, cm):
    kdim = A_HEAD_DIM
    nl = s0.shape[-1]
    nbk = LANES // dup
    pspec = pl.BlockSpec((kdim, LANES), lambda g, tb: (0, g))
    sspec = pl.BlockSpec((kdim, nv, LANES), lambda g, tb: (0, 0, g))
    scratch = [pltpu.VMEM((tl * kdim, LANES), F32)] * 5
    if cm:
        assert nl == LANES and tl == LANES
        inspec = pl.BlockSpec((nbk * kdim, tl), lambda g, tb: (0, tb), pipeline_mode=pl.Buffered(1))
        in_specs = [inspec] * 5
        yspec = pl.BlockSpec((nbk * kdim, tl), lambda g, tb: (0, tb))
        yshape = jax.ShapeDtypeStruct((nbk * kdim, length), F32)
        scratch += [pltpu.VMEM((tl * nv, LANES), F32)] * 2
        est = 5 * nbk * kdim * tl * 4 + 2 * nbk * kdim * tl * 4 + 2 * tl * nv * LANES * 4
    else:
        kspec = pl.BlockSpec((tl, kdim, nbk), lambda g, tb: (tb, 0, g))
        yspec = pl.BlockSpec((tl, nv, LANES), lambda g, tb: (tb, 0, g))
        in_specs = [kspec] * 4 + [yspec]
        yshape = jax.ShapeDtypeStruct((length, nv, nl), F32)
        est = 2 * (4 * tl * kdim * LANES * 4 + 2 * tl * nv * LANES * 4) + 4 * tl * kdim * LANES * 4
    est += 5 * kdim * nv * LANES * 4 + 5 * tl * kdim * LANES * 4
    return pl.pallas_call(
        functools.partial(_rwkv_scan_kernel, tl=tl, nv=nv, dup=dup, cm=cm),
        grid=(nl // LANES, length // tl),
        in_specs=in_specs + [pspec, pspec,
                             pl.BlockSpec((kdim, nv, LANES), lambda g, tb: (0, 0, g),
                                          pipeline_mode=pl.Buffered(1))],
        out_specs=[yspec, sspec],
        out_shape=[yshape, jax.ShapeDtypeStruct((kdim, nv, nl), F32)],
        scratch_shapes=scratch,
        compiler_params=_params(("arbitrary", "arbitrary"), est),
        name="rwkv_scan",
    )(*ops, kkp, kap, s0)


def _rwkv_recurrence(ops, k_k, k_a, wkv0, bn, length, cm):
    n = bn * A_HEADS
    dup = max(1, LANES // n)
    nv = A_HEAD_DIM // dup
    assert nv % SUBLANES == 0 and (n * dup) % LANES == 0, (bn, n)

    def to_t(x):
        x = x.reshape(bn, length, A_HEAD_DIM, A_HEADS)
        return jnp.transpose(x, (1, 2, 0, 3)).reshape(length, A_HEAD_DIM, n)

    def param_t(p):
        p = jnp.broadcast_to(p.reshape(1, A_HEADS, A_HEAD_DIM), (bn, A_HEADS, A_HEAD_DIM))
        p = jnp.transpose(p, (2, 0, 1)).reshape(A_HEAD_DIM, n).astype(F32)
        return jnp.tile(p, (1, dup))

    def vpack(x):
        lead = x.shape[0]
        x = x.reshape(lead, dup, nv, n)
        return jnp.transpose(x, (0, 2, 1, 3)).reshape(lead, nv, dup * n)

    def vunpack(x):
        lead = x.shape[0]
        x = x.reshape(lead, nv, dup, n)
        return jnp.transpose(x, (0, 2, 1, 3)).reshape(lead, A_HEAD_DIM, n)

    s0 = jnp.transpose(wkv0.astype(F32), (3, 2, 0, 1)).reshape(A_HEAD_DIM, A_HEAD_DIM, n)
    s0 = vpack(s0)
    if cm:
        y, s_last = _rwkv_scan(ops, param_t(k_k), param_t(k_a), s0, length, LANES, nv, dup, True)
    else:
        r, w, k, a, v = ops
        ops_t = [to_t(x) for x in (r, w, k, a)] + [vpack(to_t(v))]
        tl = _tile(length, 32, 1)
        y_t, s_last = _rwkv_scan(ops_t, param_t(k_k), param_t(k_a), s0, length, tl, nv, dup, False)
        y = vunpack(y_t).reshape(length, A_HEAD_DIM, bn, A_HEADS)
        y = jnp.transpose(y, (2, 0, 1, 3)).reshape(bn * length, A_WIDTH)
    s_last = vunpack(s_last).reshape(A_HEAD_DIM, A_HEAD_DIM, bn, A_HEADS)
    s_last = jnp.transpose(s_last, (2, 3, 1, 0))
    return y, s_last


def _retention_kernel(p_ref, cos_ref, sin_ref, qdec_ref, kdec_ref, intra_ref, cdec_ref, s0_ref,
                      y_ref, sout_ref, s_scr, *, nsub, c, chained):
    ci = pl.program_id(1)
    rows = nsub * c

    if chained:
        @pl.when(ci == 0)
        def _():
            s_scr[...] = s0_ref[...]

    lane = lax.broadcasted_iota(jnp.int32, (rows, 2 * B_QK_WIDTH), 1)
    half = B_QK_DIM // 2
    qk = p_ref[:, 0:2 * B_QK_WIDTH]
    swapped = jnp.where((lane % B_QK_DIM) < half, pltpu.roll(qk, 2 * B_QK_WIDTH - half, 1),
                        pltpu.roll(qk, half, 1))
    rot = qk * cos_ref[...] + swapped * sin_ref[...]
    q = rot[:, :B_QK_WIDTH] * (B_QK_DIM ** -0.5)
    k = rot[:, B_QK_WIDTH:]
    qd = (q * qdec_ref[...]).astype(BF16)
    kd = (k * kdec_ref[...]).astype(BF16)
    qb = q.astype(BF16)
    kb = k.astype(BF16)
    for h in range(B_HEADS):
        qs = slice(h * B_QK_DIM, (h + 1) * B_QK_DIM)
        vs = slice(2 * B_QK_WIDTH + h * B_V_DIM, 2 * B_QK_WIDTH + (h + 1) * B_V_DIM)
        gs = slice(2 * B_QK_WIDTH + B_V_WIDTH + h * B_V_DIM,
                   2 * B_QK_WIDTH + B_V_WIDTH + (h + 1) * B_V_DIM)
        if chained:
            st = s_scr[0, h]
        for u in range(nsub):
            rs = slice(u * c, (u + 1) * c)
            if not chained:
                st = s0_ref[u, h]
            vh = p_ref[rs, vs].astype(BF16)
            gh = p_ref[rs, gs]
            scores = lax.dot_general(qb[rs, qs], kb[rs, qs], (((1,), (1,)), ((), ())),
                                     preferred_element_type=F32) * intra_ref[h]
            o = _dot(scores.astype(BF16), vh) + _dot(qd[rs, qs], st.astype(BF16))
            st = st * cdec_ref[h] + lax.dot_general(
                kd[rs, qs], vh, (((0,), (0,)), ((), ())), preferred_element_type=F32)
            o = o * lax.rsqrt(jnp.mean(o * o, -1, keepdims=True) + NORM_EPS)
            y_ref[rs, h * B_V_DIM:(h + 1) * B_V_DIM] = o * (gh * _sigmoid(gh))
            if not chained:
                sout_ref[u, h] = st
        if chained:
            s_scr[0, h] = st

    if chained:
        @pl.when(ci == pl.num_programs(1) - 1)
        def _():
            sout_ref[...] = s_scr[...]


def _retention(pb, s0, bn, length, pos0):
    c = math.gcd(length, RET_CHUNK)
    nc = length // c
    chained = nc > 1
    nsub = _tile(nc, 8, 1) if chained else _tile(bn, 8, 1)
    sb = 1 if chained else nsub
    nstep = nc // nsub if chained else 1
    half = B_QK_DIM // 2
    inv = ROPE_BASE ** (-jnp.arange(half, dtype=F32) / half)
    pos = (pos0 + jnp.arange(length)).astype(F32)
    ang = pos[:, None] * inv[None, :]
    cos, sin = jnp.cos(ang), jnp.sin(ang)
    cosf = jnp.tile(jnp.concatenate([cos, cos], -1), (1, 2 * B_HEADS))
    sinf = jnp.tile(jnp.concatenate([-sin, sin], -1), (1, 2 * B_HEADS))
    log_g = jnp.log1p(-jnp.exp2(-5.0 - jnp.arange(B_HEADS, dtype=F32)))
    idx = jnp.arange(c, dtype=F32)
    diff = idx[:, None] - idx[None, :]
    intra = jnp.where(diff >= 0, jnp.exp(jnp.maximum(diff, 0.0) * log_g[:, None, None]), 0.0)
    q_dec = jnp.exp((idx + 1.0)[:, None] * log_g[None, :])
    k_dec = jnp.exp((c - 1.0 - idx)[:, None] * log_g[None, :])
    c_dec = jnp.exp(c * log_g)
    rows = nsub * c
    qdec = jnp.tile(jnp.repeat(q_dec, B_QK_DIM, axis=1), (nsub, 1))
    kdec = jnp.tile(jnp.repeat(k_dec, B_QK_DIM, axis=1), (nsub, 1))
    cdec = jnp.broadcast_to(c_dec[:, None, None], (B_HEADS, 1, B_V_DIM))
    if not chained:
        cosf = jnp.tile(cosf, (nsub, 1))
        sinf = jnp.tile(sinf, (nsub, 1))

    full = lambda shape: pl.BlockSpec(shape, lambda i, j: (0,) * len(shape))
    tspec = pl.BlockSpec((rows, 2 * B_QK_WIDTH), lambda i, j: (j, 0))
    sspec = pl.BlockSpec((sb, B_HEADS, B_QK_DIM, B_V_DIM), lambda i, j: (i, 0, 0, 0))
    est = 2 * (rows * B_COLS * 4 + rows * B_V_WIDTH * 4 + 2 * sb * B_HEADS * B_QK_DIM * B_V_DIM * 4
               + 2 * rows * 2 * B_QK_WIDTH * 4) \
        + 3 * sb * B_HEADS * B_QK_DIM * B_V_DIM * 4 + 8 * rows * B_COLS * 4
    return pl.pallas_call(
        functools.partial(_retention_kernel, nsub=nsub, c=c, chained=chained),
        grid=(bn // sb, nstep),
        in_specs=[pl.BlockSpec((rows, B_COLS), lambda i, j: (i * nstep + j, 0)),
                  tspec, tspec,
                  full((rows, B_QK_WIDTH)), full((rows, B_QK_WIDTH)), full((B_HEADS, c, c)),
                  full((B_HEADS, 1, B_V_DIM)), sspec],
        out_specs=[pl.BlockSpec((rows, B_V_WIDTH), lambda i, j: (i * nstep + j, 0)), sspec],
        out_shape=[jax.ShapeDtypeStruct((bn * length, B_V_WIDTH), F32),
                   jax.ShapeDtypeStruct((bn, B_HEADS, B_QK_DIM, B_V_DIM), F32)],
        scratch_shapes=[pltpu.VMEM((sb, B_HEADS, B_QK_DIM, B_V_DIM), F32)],
        compiler_params=_params(("arbitrary", "arbitrary"), est),
        name="retention",
    )(pb, cosf, sinf, qdec, kdec, intra, cdec, s0.astype(F32))


def _mix_out_kernel(x_ref, y_ref, r_ref, k_ref, a_ref, v_ref, g_ref, yb_ref, ka_ref, rk_ref, lg_ref,
                    lb_ref, ones_ref, w_ref, o_ref, *, y_cm):
    ones = ones_ref[...]
    y = y_ref[...].T if y_cm else y_ref[...]
    inv_d = 1.0 / A_HEAD_DIM
    mean = _seg_sum(y, ones) * inv_d
    d = y - mean
    var = _seg_sum(d * d, ones) * inv_d
    yn = d * lax.rsqrt(var + A_GN_EPS) * lg_ref[...] + lb_ref[...]
    v = v_ref[...]
    kmod = k_ref[...] * (1.0 + (a_ref[...] - 1.0) * ka_ref[...])
    bonus = _seg_sum(r_ref[...] * kmod * rk_ref[...], ones) * v
    ya = ((yn + bonus) * g_ref[...]).astype(BF16)
    o_ref[...] = (x_ref[...] + _dot(ya, w_ref[0:A_WIDTH, :])
                  + _dot(yb_ref[...].astype(BF16), w_ref[A_WIDTH:, :]))


def _mix_out(x, y, r, k, a, v, g, yb, k_a, r_k, lnx_g, lnx_b, ones, w_out, seq_len, y_cm):
    t, d = x.shape
    tm = _tile(seq_len, 512, LANES) if y_cm else _tile(t, 512)
    row = lambda n: pl.BlockSpec((tm, n), lambda i: (i, 0))
    full = lambda shape: pl.BlockSpec(shape, lambda i: (0,) * len(shape))
    est = 2 * (2 * tm * d * 4 + 7 * tm * A_WIDTH * 4 + (A_WIDTH + B_V_WIDTH) * d * 2) + 10 * tm * A_WIDTH * 4
    vec = lambda p: p.reshape(1, A_WIDTH)
    nt = seq_len // tm if y_cm else 1
    yspec = pl.BlockSpec((A_WIDTH, tm), lambda i: (i // nt, i % nt)) if y_cm else row(A_WIDTH)
    return pl.pallas_call(
        functools.partial(_mix_out_kernel, y_cm=y_cm),
        grid=(t // tm,),
        in_specs=[row(d), yspec] + [row(A_WIDTH)] * 6 + [full((1, A_WIDTH))] * 4
                 + [full((A_WIDTH, A_WIDTH)), full((A_WIDTH + B_V_WIDTH, d))],
        out_specs=row(d),
        out_shape=jax.ShapeDtypeStruct((t, d), F32),
        compiler_params=_params(("arbitrary",), est),
        name="mix_out",
    )(x, y, r, k, a, v, g, yb, vec(k_a), vec(r_k), vec(lnx_g), vec(lnx_b), ones, w_out)


def _ffn_kernel(x_ref, g_ref, wg_ref, wu_ref, wd_ref, o_ref, h_scr):
    j = pl.program_id(1)

    @pl.when(j == 0)
    def _():
        x = x_ref[...]
        h_scr[...] = _rms(x, g_ref[...]).astype(BF16)
        o_ref[...] = x

    h = h_scr[...]
    a = _dot(h, wg_ref[...])
    b = _dot(h, wu_ref[...])
    m = (a * _sigmoid(a) * b).astype(BF16)
    o_ref[...] += _dot(m, wd_ref[...])


def _ffn(x, g, wg, wu, wd):
    t, d = x.shape
    ff = wg.shape[1]
    tm = _tile(t, 512)
    tf = _tile(ff, 1536, LANES)
    est = 2 * (2 * tm * d * 4 + 3 * d * tf * 2) + tm * d * 2 + 3 * tm * tf * 4
    return pl.pallas_call(
        _ffn_kernel,
        grid=(t // tm, ff // tf),
        in_specs=[pl.BlockSpec((tm, d), lambda i, j: (i, 0)),
                  pl.BlockSpec((1, d), lambda i, j: (0, 0)),
                  pl.BlockSpec((d, tf), lambda i, j: (0, j)),
                  pl.BlockSpec((d, tf), lambda i, j: (0, j)),
                  pl.BlockSpec((tf, d), lambda i, j: (j, 0))],
        out_specs=pl.BlockSpec((tm, d), lambda i, j: (i, 0)),
        out_shape=jax.ShapeDtypeStruct((t, d), F32),
        scratch_shapes=[pltpu.VMEM((tm, d), BF16)],
        compiler_params=_params(("arbitrary", "arbitrary"), est),
        name="ffn",
    )(x, g.reshape(1, d), wg, wu, wd)


def _rglru_kernel(gate_ref, xb_ref, tail0_ref, h0_ref, cw_ref, cb_ref, wa_ref, ba_ref, wx_ref,
                  bx_ref, lam_ref, y_ref, hlast_ref, tail_scr, h_scr, a_scr, b_scr, hs_scr, *, tl):
    tb = pl.program_id(1)

    @pl.when(tb == 0)
    def _():
        tail_scr[...] = tail0_ref[0]
        h_scr[...] = h0_ref[0]

    xb = xb_ref[...]
    full = jnp.concatenate([tail_scr[...], xb], axis=0)
    xc = cb_ref[...]
    for j in range(CONV_W):
        shift = CONV_W - 1 - j
        term = pltpu.roll(full, shift, 0) if shift else full
        xc = xc + term[SUBLANES:, :] * cw_ref[j:j + 1, :]
    tail_scr[...] = xb[tl - SUBLANES:, :]
    xcb = xc.astype(BF16)
    for n in range(C_BLOCKS):
        sl = slice(n * C_BLOCK_DIM, (n + 1) * C_BLOCK_DIM)
        xn = xcb[:, sl]
        r = _sigmoid(_dot(xn, wa_ref[n]) + ba_ref[:, sl])
        i = _sigmoid(_dot(xn, wx_ref[n]) + bx_ref[:, sl])
        lam = lam_ref[:, sl]
        softplus_neg_lam = jnp.maximum(-lam, 0.0) + jnp.log1p(jnp.exp(-jnp.abs(lam)))
        log_a = -LRU_C * r * softplus_neg_lam
        a = jnp.exp(log_a)
        gain = jnp.sqrt(-jnp.tanh(log_a) * (a * a + 1.0))
        a_scr[:, sl] = a
        b_scr[:, sl] = gain * i * xc[:, sl]

    def row(t, h):
        h = a_scr[pl.ds(t, 1), :] * h + b_scr[pl.ds(t, 1), :]
        hs_scr[pl.ds(t, 1), :] = h
        return h

    h = lax.fori_loop(0, tl, row, h_scr[...], unroll=8)
    h_scr[...] = h
    hlast_ref[0] = h
    gate = gate_ref[...]
    cdf = 0.5 * (1.0 + jnp.tanh(math.sqrt(2.0 / math.pi) * (gate + 0.044715 * (gate * gate * gate))))
    y_ref[...] = (gate * cdf * hs_scr[...]).astype(BF16)


def _rglru(gate, xb, tail0, h0, wts, bn, length):
    tl = _tile(length, 256)
    nt = length // tl
    full = lambda shape: pl.BlockSpec(shape, lambda i, j: (0,) * len(shape))
    row = pl.BlockSpec((tl, D_RNN), lambda i, j: (i * nt + j, 0))
    est = 2 * (3 * tl * D_RNN * 4 + 2 * C_BLOCKS * C_BLOCK_DIM * C_BLOCK_DIM * 2) + 10 * tl * D_RNN * 4
    return pl.pallas_call(
        functools.partial(_rglru_kernel, tl=tl),
        grid=(bn, nt),
        in_specs=[row, row,
                  pl.BlockSpec((1, SUBLANES, D_RNN), lambda i, j: (i, 0, 0)),
                  pl.BlockSpec((1, 1, D_RNN), lambda i, j: (i, 0, 0)),
                  full((CONV_W, D_RNN)), full((1, D_RNN)),
                  full((C_BLOCKS, C_BLOCK_DIM, C_BLOCK_DIM)), full((1, D_RNN)),
                  full((C_BLOCKS, C_BLOCK_DIM, C_BLOCK_DIM)), full((1, D_RNN)),
                  full((1, D_RNN))],
        out_specs=[row, pl.BlockSpec((1, 1, D_RNN), lambda i, j: (i, 0, 0))],
        out_shape=[jax.ShapeDtypeStruct((bn * length, D_RNN), BF16),
                   jax.ShapeDtypeStruct((bn, 1, D_RNN), F32)],
        scratch_shapes=[pltpu.VMEM((SUBLANES, D_RNN), F32), pltpu.VMEM((1, D_RNN), F32),
                        pltpu.VMEM((tl, D_RNN), F32), pltpu.VMEM((tl, D_RNN), F32),
                        pltpu.VMEM((tl, D_RNN), F32)],
        compiler_params=_params(("arbitrary", "arbitrary"), est),
        name="rglru",
    )(gate, xb, tail0, h0, *wts)


def _matmul_res_kernel(y_ref, w_ref, x_ref, o_ref):
    o_ref[...] = x_ref[...] + _dot(y_ref[...], w_ref[...])


def _matmul_res(y, w, x):
    t, d = x.shape
    kdim = y.shape[1]
    tm = _tile(t, 512, 16)
    est = 2 * (tm * kdim * 2 + kdim * d * 2 + 2 * tm * d * 4)
    return pl.pallas_call(
        _matmul_res_kernel,
        grid=(t // tm,),
        in_specs=[pl.BlockSpec((tm, kdim), lambda i: (i, 0)),
                  pl.BlockSpec((kdim, d), lambda i: (0, 0)),
                  pl.BlockSpec((tm, d), lambda i: (i, 0))],
        out_specs=pl.BlockSpec((tm, d), lambda i: (i, 0)),
        out_shape=jax.ShapeDtypeStruct((t, d), F32),
        compiler_params=_params(("arbitrary",), est),
        name="matmul_res",
    )(y, w, x)


def _router_kernel(x_ref, g_ref, wr_ref, h_ref, route_ref, *, tm):
    h = _rms(x_ref[...], g_ref[...])
    h_ref[...] = h
    logits = jnp.dot(h, wr_ref[...], preferred_element_type=F32, precision=lax.Precision.HIGHEST)
    lane = lax.broadcasted_iota(jnp.int32, (tm, LANES), 1).astype(F32)
    neg = jnp.float32(-jnp.inf)
    lg = jnp.where(lane < N_EXPERTS, logits, neg)
    m1 = jnp.max(lg, -1, keepdims=True)
    i1 = jnp.min(jnp.where(lg == m1, lane, float(LANES)), -1, keepdims=True)
    lg2 = jnp.where(lane == i1, neg, lg)
    m2 = jnp.max(lg2, -1, keepdims=True)
    i2 = jnp.min(jnp.where(lg2 == m2, lane, float(LANES)), -1, keepdims=True)
    e = jnp.exp(m2 - m1)
    g1 = 1.0 / (1.0 + e)
    g2 = e / (1.0 + e)
    route_ref[...] = jnp.where(lane == 0, i1, jnp.where(lane == 1, i2, jnp.where(
        lane == 2, g1, jnp.where(lane == 3, g2, 0.0))))


def _router(x, g, wr_pad):
    t, d = x.shape
    tm = _tile(t, 512)
    est = 2 * (2 * tm * d * 4 + d * LANES * 4 + tm * LANES * 4) + 2 * tm * d * 4
    return pl.pallas_call(
        functools.partial(_router_kernel, tm=tm),
        grid=(t // tm,),
        in_specs=[pl.BlockSpec((tm, d), lambda i: (i, 0)),
                  pl.BlockSpec((1, d), lambda i: (0, 0)),
                  pl.BlockSpec((d, LANES), lambda i: (0, 0))],
        out_specs=[pl.BlockSpec((tm, d), lambda i: (i, 0)),
                   pl.BlockSpec((tm, LANES), lambda i: (i, 0))],
        out_shape=[jax.ShapeDtypeStruct((t, d), F32), jax.ShapeDtypeStruct((t, LANES), F32)],
        compiler_params=_params(("arbitrary",), est),
        name="router",
    )(x, g.reshape(1, d), wr_pad)


def _moe_kernel(be_ref, bv_ref, tok_hbm, h_hbm, wg_ref, wu_ref, wd_ref, o_ref,
                idx_smem, xbuf, xbf, sem_idx, sem_rows, *, tm):
    i = pl.program_id(0)
    j = pl.program_id(1)
    nb = pl.num_programs(0)
    valid = bv_ref[i] != 0
    slot = i % 2
    nxt = jnp.minimum(i + 1, nb - 1)

    def idx_copy(blk, s):
        return pltpu.make_async_copy(tok_hbm.at[pl.ds(blk * tm, tm)],
                                     idx_smem.at[pl.ds(s * tm, tm)], sem_idx.at[s])

    def row_copy(tok, s, g, u):
        return pltpu.make_async_copy(h_hbm.at[pl.ds(tok, 1)], xbuf.at[s, g, pl.ds(u, 1)],
                                     sem_rows.at[s])

    def issue_rows(s):
        def issue(g, c):
            for u in range(SUBLANES):
                row_copy(idx_smem[s * tm + g * SUBLANES + u], s, g, u).start()
            return c
        lax.fori_loop(0, tm // SUBLANES, issue, 0)

    def drain_rows(s):
        def drain(g, c):
            for u in range(SUBLANES):
                row_copy(0, s, g, u).wait()
            return c
        lax.fori_loop(0, tm // SUBLANES, drain, 0)

    def block_start(s):
        ahead = idx_copy(nxt, 1 - s)
        ahead.start()
        drain_rows(s)
        xbf[...] = xbuf[s].reshape(xbf.shape).astype(BF16)
        ahead.wait()
        issue_rows(1 - s)

    @pl.when(jnp.logical_and(valid, j == 0))
    def _():
        @pl.when(i == 0)
        def _():
            first = idx_copy(0, 0)
            first.start()
            first.wait()
            issue_rows(0)

        for s in range(2):
            pl.when(slot == s)(functools.partial(block_start, s))

    @pl.when(valid)
    def _():
        x = xbf[...]
        a = _dot(x, wg_ref[...])
        b = _dot(x, wu_ref[...])
        m = (a * _sigmoid(a) * b).astype(BF16)
        contrib = _dot(m, wd_ref[...])

        @pl.when(j == 0)
        def _():
            o_ref[...] = contrib

        @pl.when(j > 0)
        def _():
            o_ref[...] += contrib

    is_last = jnp.logical_or(i == nb - 1, bv_ref[nxt] == 0)

    @pl.when(jnp.logical_and(jnp.logical_and(valid, is_last), j == pl.num_programs(1) - 1))
    def _():
        for s in range(2):
            pl.when(slot == s)(functools.partial(drain_rows, 1 - s))

    @pl.when(jnp.logical_and(jnp.logical_not(valid), j == 0))
    def _():
        o_ref[...] = jnp.zeros_like(o_ref)


def _moe(block_e, block_valid, tok_sorted, h, wg, wu, wd, tm):
    p = tok_sorted.shape[0]
    d = h.shape[1]
    ff = wg.shape[2]
    tf = _tile(ff, 1792, LANES)
    nf = ff // tf
    nb = p // tm

    def wcol(i, j, be, bv):
        return (be[i], 0, jnp.where(bv[i] != 0, j, nf - 1))

    def wrow(i, j, be, bv):
        return (be[i], jnp.where(bv[i] != 0, j, nf - 1), 0)

    est = 2 * (3 * d * tf * 2 + tm * d * 4) + 2 * tm * d * 4 + tm * d * 2 + 3 * tm * tf * 4
    grid_spec = pltpu.PrefetchScalarGridSpec(
        num_scalar_prefetch=2,
        grid=(nb, nf),
        in_specs=[pl.BlockSpec(memory_space=pl.ANY),
                  pl.BlockSpec(memory_space=pl.ANY),
                  pl.BlockSpec((None, d, tf), wcol),
                  pl.BlockSpec((None, d, tf), wcol),
                  pl.BlockSpec((None, tf, d), wrow)],
        out_specs=pl.BlockSpec((tm, d), lambda i, j, be, bv: (i, 0)),
        scratch_shapes=[pltpu.SMEM((2 * tm,), jnp.int32),
                        pltpu.VMEM((2, tm // SUBLANES, SUBLANES, d), F32),
                        pltpu.VMEM((tm, d), BF16), pltpu.SemaphoreType.DMA((2,)),
                        pltpu.SemaphoreType.DMA((2,))],
    )
    return pl.pallas_call(
        functools.partial(_moe_kernel, tm=tm),
        grid_spec=grid_spec,
        out_shape=jax.ShapeDtypeStruct((p, d), F32),
        compiler_params=_params(("arbitrary", "arbitrary"), est),
        name="moe",
    )(block_e, block_valid, tok_sorted, h, wg, wu, wd)


def _combine_kernel(pos_hbm, y_hbm, x_ref, route_ref, g_ref, o_ref, idx_smem, ybuf, sem_idx,
                    sem_rows, *, tm):
    i = pl.program_id(0)
    slot = i % 2
    nrows = 2 * tm

    def row_copy(src, s, g, u):
        return pltpu.make_async_copy(y_hbm.at[pl.ds(src, 1)], ybuf.at[s, g, pl.ds(u, 1)],
                                     sem_rows.at[s])

    def fetch(tile, s):
        idx_copy = pltpu.make_async_copy(pos_hbm.at[pl.ds(tile * nrows, nrows)],
                                         idx_smem.at[pl.ds(s * nrows, nrows)], sem_idx.at[s])
        idx_copy.start()
        idx_copy.wait()

        def issue(g, c):
            for u in range(SUBLANES):
                row_copy(idx_smem[s * nrows + g * SUBLANES + u], s, g, u).start()
            return c

        lax.fori_loop(0, nrows // SUBLANES, issue, 0)

    @pl.when(i == 0)
    def _():
        fetch(0, 0)

    def tile_body(s):
        @pl.when(i + 1 < pl.num_programs(0))
        def _():
            fetch(i + 1, 1 - s)

        def drain(g, c):
            for u in range(SUBLANES):
                row_copy(0, s, g, u).wait()
            return c

        lax.fori_loop(0, nrows // SUBLANES, drain, 0)
        route = route_ref[...]
        g1 = route[:, TOP_K:TOP_K + 1]
        g2 = route[:, TOP_K + 1:TOP_K + 2]
        nt = tm // SUBLANES
        y1 = ybuf[s, 0:nt].reshape(x_ref.shape)
        y2 = ybuf[s, nt:2 * nt].reshape(x_ref.shape)
        x = x_ref[...] + g1 * y1 + g2 * y2
        o_ref[...] = _rms(x, g_ref[...])

    for s in range(2):
        pl.when(slot == s)(functools.partial(tile_body, s))


def _combine(pos, y_sorted, x, route, g, tm):
    t, d = x.shape
    est = 2 * (2 * tm * d * 4 + tm * LANES * 4) + 4 * tm * d * 4 + 2 * tm * d * 4
    return pl.pallas_call(
        functools.partial(_combine_kernel, tm=tm),
        grid=(t // tm,),
        in_specs=[pl.BlockSpec(memory_space=pl.ANY),
                  pl.BlockSpec(memory_space=pl.ANY),
                  pl.BlockSpec((tm, d), lambda i: (i, 0)),
                  pl.BlockSpec((tm, LANES), lambda i: (i, 0)),
                  pl.BlockSpec((1, d), lambda i: (0, 0))],
        out_specs=pl.BlockSpec((tm, d), lambda i: (i, 0)),
        out_shape=jax.ShapeDtypeStruct((t, d), F32),
        scratch_shapes=[pltpu.SMEM((4 * tm,), jnp.int32),
                        pltpu.VMEM((2, 2 * tm // SUBLANES, SUBLANES, d), F32),
                        pltpu.SemaphoreType.DMA((2,)), pltpu.SemaphoreType.DMA((2,))],
        compiler_params=_params(("arbitrary",), est),
        name="combine",
    )(pos, y_sorted, x, route, g.reshape(1, d))


def _route_plan(route, tm):
    t = route.shape[0]
    flat_e = route[:, 0:TOP_K].astype(jnp.int32).reshape(-1)
    flat_tok = jnp.repeat(jnp.arange(t, dtype=jnp.int32), TOP_K)
    onehot = (flat_e[:, None] == jnp.arange(N_EXPERTS, dtype=jnp.int32)[None, :]).astype(jnp.int32)
    csum = jnp.cumsum(onehot, axis=0)
    counts = csum[-1]
    rank = jnp.sum((csum - onehot) * onehot, axis=1)
    padded = ((counts + tm - 1) // tm) * tm
    pend = jnp.cumsum(padded)
    pstart = pend - padded
    dest = pstart[flat_e] + rank
    nb = (t * TOP_K + tm - 1) // tm + N_EXPERTS
    p = nb * tm
    tok_sorted = jnp.zeros((p,), jnp.int32).at[dest].set(flat_tok)
    block_start = jnp.arange(nb, dtype=jnp.int32) * tm
    block_valid = (block_start < pend[-1]).astype(jnp.int32)
    last_e = jnp.sum((pend <= pend[-1] - 1).astype(jnp.int32))
    block_e = jnp.sum((pend[None, :] <= block_start[:, None]).astype(jnp.int32), axis=1)
    block_e = jnp.where(block_valid != 0, jnp.minimum(block_e, N_EXPERTS - 1), last_e)
    return tok_sorted, block_e.astype(jnp.int32), block_valid, dest.reshape(t, TOP_K)


def _tile_pos(dest, tm):
    t = dest.shape[0]
    return jnp.transpose(dest.reshape(t // tm, tm, TOP_K), (0, 2, 1)).reshape(-1)


def _rwkv_perm():
    j = np.arange(A_WIDTH)
    return (j % A_HEADS) * A_HEAD_DIM + j // A_HEADS


def _rwkv_cols():
    perm = _rwkv_perm()
    return np.concatenate([perm, A_WIDTH + perm, 2 * A_WIDTH + perm,
                           np.arange(3 * A_WIDTH, A_COLS)])


def _prep_weights(w):
    bf = lambda a: a.astype(BF16)
    perm = _rwkv_perm()
    cols = _rwkv_cols()
    head = np.arange(A_WIDTH) % A_HEADS
    lora = jnp.zeros((A_LORA_COLS, A_WIDTH), F32)
    out = dict(w)
    out['ones'] = jnp.asarray(head[:, None] == head[None, :], BF16)
    out['wdec_pad'] = bf(lora.at[0:64].set(w['a_w_decay'][0])[:, perm])
    out['wiclr_pad'] = bf(lora.at[64:128].set(w['a_w_iclr'][0])[:, perm])
    out['wgate_pad'] = bf(lora.at[128:256].set(w['a_w_gate'][0])[:, perm])
    out['mu_p'] = w['a_mu'][0][cols].reshape(1, -1)
    for name in ('a_w0', 'a_a0', 'a_k_a', 'a_r_k', 'a_lnx_g', 'a_lnx_b'):
        out[name + '_p'] = w[name][0].reshape(-1)[perm]
    for name in ('ffn_gate', 'ffn_up', 'ffn_down', 'w_in1', 'w_out1', 'c_w_a',
                 'c_w_x', 'moe_gate', 'moe_up', 'moe_down'):
        out[name] = bf(w[name][0])
    in_cols = np.concatenate([cols, np.arange(A_COLS, A_COLS + B_COLS)])
    out['w_in0'] = bf(w['w_in0'][0][:, in_cols])
    out_rows = np.concatenate([perm, np.arange(A_WIDTH, A_WIDTH + B_V_WIDTH)])
    out['w_out0'] = bf(w['w_out0'][0][out_rows])
    out['router_pad'] = jnp.zeros((w['moe_router'].shape[1], LANES), F32).at[:, :N_EXPERTS].set(
        w['moe_router'][0])
    return out


def _layer0(x, bn, length, pos0, shift, wkv, ret, w):
    t = bn * length
    cols = _rwkv_cols()
    pa, pb = _norm_matmul(x, w['norm_mix0'][0], w['w_in0'], (A_COLS, B_COLS))
    tm = _tile(t, 256) if length >= 256 else _tile(t, 256, length)
    shift = shift.astype(F32)[:, cols]
    if length >= tm:
        starts = jnp.arange(t // tm) * tm
        before = pa[jnp.maximum(starts - 1, 0)]
        first = jnp.where((starts % length == 0)[:, None], shift[starts // length], before)
        ext = jnp.zeros((t // tm, SUBLANES, A_COLS), F32).at[:, 0].set(first).reshape(-1, A_COLS)
    else:
        ext = jnp.repeat(shift, length, axis=0)
    row = lambda a: a.reshape(1, -1)
    cm = bn * A_HEADS <= LANES and length % LANES == 0 and tm % LANES == 0 and length >= tm
    outs = _rwkv_prep(
        pa, ext, length, tm,
        (w['mu_p'], row(w['a_w0_p']), w['wdec_pad'], row(w['a_a0_p']), w['wiclr_pad'],
         w['wgate_pad']), cm)
    r, dec, k, a, v, g = outs[:6]
    ops = outs[6:] if cm else (r, dec, k, a, v)
    y, wkv_new = _rwkv_recurrence(ops, w['a_k_k'][0], w['a_k_a'][0], wkv, bn, length, cm)
    yb, ret_new = _retention(pb, ret, bn, length, pos0)
    x = _mix_out(x, y, r, k, a, v, g, yb, w['a_k_a_p'], w['a_r_k_p'], w['a_lnx_g_p'],
                 w['a_lnx_b_p'], w['ones'], w['w_out0'], length, cm)
    x = _ffn(x, w['norm_ffn0'][0], w['ffn_gate'], w['ffn_up'], w['ffn_down'])
    shift_new = pa.reshape(bn, length, A_COLS)[:, -1][:, np.argsort(cols)]
    return x, shift_new, wkv_new, ret_new


def _layer1_mixer(x, bn, length, conv, hlru, w):
    assert length >= CONV_W - 1
    gate, xb = _norm_matmul(x, w['norm_mix1'][0], w['w_in1'], (D_RNN, D_RNN))
    tail0 = jnp.zeros((bn, SUBLANES, D_RNN), F32).at[:, SUBLANES - (CONV_W - 1):].set(conv.astype(F32))
    row = lambda a: a[0].reshape(1, -1)
    y, h_last = _rglru(gate, xb, tail0, hlru.astype(F32).reshape(bn, 1, D_RNN),
                       (w['c_conv_w'][0], row(w['c_conv_b']), w['c_w_a'], row(w['c_b_a']),
                        w['c_w_x'], row(w['c_b_x']), row(w['c_lambda'])), bn, length)
    x = _matmul_res(y, w['w_out1'], x)
    conv_new = xb.reshape(bn, length, D_RNN)[:, length - (CONV_W - 1):]
    return x, conv_new, h_last.reshape(bn, D_RNN)


def kernel(x_prompt, x_sample, state_rwkv_shift, state_rwkv_wkv, state_ret, state_lru_conv, state_lru_h, norm_mix0, w_in0, a_mu, a_w0, a_w_decay, a_a0, a_w_iclr, a_w_gate, a_k_k, a_k_a, a_r_k, a_lnx_g, a_lnx_b, w_out0, norm_ffn0, ffn_gate, ffn_up, ffn_down, norm_mix1, w_in1, c_conv_w, c_conv_b, c_w_a, c_b_a, c_w_x, c_b_x, c_lambda, w_out1, norm_ffn1, moe_router, moe_gate, moe_up, moe_down, norm_final):
    w = _prep_weights(dict(
        norm_mix0=norm_mix0, w_in0=w_in0, a_mu=a_mu, a_w0=a_w0, a_w_decay=a_w_decay, a_a0=a_a0,
        a_w_iclr=a_w_iclr, a_w_gate=a_w_gate, a_k_k=a_k_k, a_k_a=a_k_a, a_r_k=a_r_k, a_lnx_g=a_lnx_g,
        a_lnx_b=a_lnx_b, w_out0=w_out0, norm_ffn0=norm_ffn0, ffn_gate=ffn_gate, ffn_up=ffn_up,
        ffn_down=ffn_down, norm_mix1=norm_mix1, w_in1=w_in1, c_conv_w=c_conv_w, c_conv_b=c_conv_b,
        c_w_a=c_w_a, c_b_a=c_b_a, c_w_x=c_w_x, c_b_x=c_b_x, c_lambda=c_lambda, w_out1=w_out1,
        norm_ffn1=norm_ffn1, moe_router=moe_router, moe_gate=moe_gate, moe_up=moe_up,
        moe_down=moe_down))
    dt = x_prompt.dtype
    d = x_prompt.shape[-1]
    bp, lp = x_prompt.shape[:2]
    bs, ls = x_sample.shape[:2]
    zeros = lambda shape: jnp.zeros(shape, F32)
    groups = [
        (x_prompt.reshape(bp * lp, d), bp, lp, 0, zeros((bp, A_COLS)),
         zeros((bp, A_HEADS, A_HEAD_DIM, A_HEAD_DIM)), zeros((bp, B_HEADS, B_QK_DIM, B_V_DIM)),
         zeros((bp, CONV_W - 1, D_RNN)), zeros((bp, D_RNN))),
        (x_sample.reshape(bs * ls, d), bs, ls, PAST_LEN, state_rwkv_shift[0], state_rwkv_wkv[0],
         state_ret[0], state_lru_conv[0], state_lru_h[0]),
    ]
    xs, states, hs, routes = [], [], [], []
    for x, bn, length, pos0, shift, wkv, ret, conv, hlru in groups:
        x, s_shift, s_wkv, s_ret = _layer0(x, bn, length, pos0, shift, wkv, ret, w)
        x, s_conv, s_h = _layer1_mixer(x, bn, length, conv, hlru, w)
        h, route = _router(x, w['norm_ffn1'][0], w['router_pad'])
        xs.append(x)
        hs.append(h)
        routes.append(route)
        states.append((s_shift, s_wkv, s_ret, s_conv, s_h))
    h_all = jnp.concatenate(hs, 0)
    route_all = jnp.concatenate(routes, 0)
    tm_moe = 512 if h_all.shape[0] >= 4096 else 64
    tok_sorted, block_e, block_valid, dest = _route_plan(route_all, tm_moe)
    y_sorted = _moe(block_e, block_valid, tok_sorted, h_all, w['moe_gate'], w['moe_up'],
                    w['moe_down'], tm_moe)
    outs = []
    off = 0
    for x, route in zip(xs, routes):
        t = x.shape[0]
        tm = _tile(t, 256)
        pos = _tile_pos(dest[off:off + t], tm)
        outs.append(_combine(pos, y_sorted, x, route, norm_final, tm))
        off += t
    y_prompt = outs[0].reshape(bp, lp, d)
    y_sample = outs[1].reshape(bs, ls, d)
    st_p = tuple(s[None].astype(dt) for s in states[0])
    st_s = tuple(s[None].astype(dt) for s in states[1])
    return (y_prompt, y_sample) + st_p + st_s
```

```python
import functools
import math

import jax
import jax.numpy as jnp
import numpy as np
from jax import lax
from jax.experimental import pallas as pl
from jax.experimental.pallas import tpu as pltpu

F32 = jnp.float32
BF16 = jnp.bfloat16

A_HEADS = 8
A_HEAD_DIM = 64
A_WIDTH = A_HEADS * A_HEAD_DIM
A_LORA_COLS = 256
A_COLS = 3 * A_WIDTH + A_LORA_COLS
A_GN_EPS = A_HEAD_DIM * 1e-5
B_HEADS = 4
B_QK_DIM = 64
B_V_DIM = 128
B_QK_WIDTH = B_HEADS * B_QK_DIM
B_V_WIDTH = B_HEADS * B_V_DIM
B_COLS = 2 * B_QK_WIDTH + 2 * B_V_WIDTH
RET_CHUNK = 64
ROPE_BASE = 10000.0
D_RNN = 1280
C_BLOCKS = 10
C_BLOCK_DIM = D_RNN // C_BLOCKS
CONV_W = 4
LRU_C = 8.0
N_EXPERTS = 8
TOP_K = 2
NORM_EPS = 1e-6
PAST_LEN = 16384

LANES = 128
SUBLANES = 8
VMEM_PHYSICAL_BYTES = 64 * 1024 * 1024
VMEM_BUDGET_BYTES = VMEM_PHYSICAL_BYTES - 4 * 1024 * 1024


def _tile(n, pref, mult=SUBLANES):
    t = min(pref, n)
    while t > mult and (n % t or t % mult):
        t -= 1
    assert n % t == 0 and t % mult == 0, (n, pref, mult)
    return t


def _params(sem, est_bytes):
    limit = int(min(max(est_bytes * 5 // 4 + (4 << 20), 32 << 20), VMEM_BUDGET_BYTES))
    return pltpu.CompilerParams(dimension_semantics=sem, vmem_limit_bytes=limit)


def _rms(x, g):
    return x * lax.rsqrt(jnp.mean(x * x, -1, keepdims=True) + NORM_EPS) * g


def _dot(a, b):
    return jnp.dot(a, b, preferred_element_type=F32)


def _sigmoid(x):
    return 0.5 * (jnp.tanh(0.5 * x) + 1.0)


def _seg_sum(x, ones):
    hi = x.astype(BF16)
    lo = (x - hi.astype(F32)).astype(BF16)
    return _dot(hi, ones) + _dot(lo, ones)


def _norm_matmul_kernel(x_ref, g_ref, w_ref, *o_refs, splits):
    h = _rms(x_ref[...], g_ref[...]).astype(BF16)
    off = 0
    for o_ref, n in zip(o_refs, splits):
        o_ref[...] = _dot(h, w_ref[:, off:off + n])
        off += n


def _norm_matmul(x, g, w, splits):
    t, d = x.shape
    n = w.shape[1]
    tm = _tile(t, 512)
    est = 2 * (tm * d * 4 + d * n * 2 + tm * n * 4) + tm * n * 4
    return pl.pallas_call(
        functools.partial(_norm_matmul_kernel, splits=splits),
        grid=(t // tm,),
        in_specs=[pl.BlockSpec((tm, d), lambda i: (i, 0)),
                  pl.BlockSpec((1, d), lambda i: (0, 0)),
                  pl.BlockSpec((d, n), lambda i: (0, 0))],
        out_specs=[pl.BlockSpec((tm, s), lambda i: (i, 0)) for s in splits],
        out_shape=[jax.ShapeDtypeStruct((t, s), F32) for s in splits],
        compiler_params=_params(("arbitrary",), est),
        name="norm_matmul",
    )(x, g.reshape(1, d), w)


def _rwkv_prep_kernel(p_ref, ext_ref, mu_ref, w0_ref, wdec_ref, a0_ref, wiclr_ref, wgate_ref,
                      r_o, w_o, k_o, a_o, v_o, g_o, *cm_outs, seq_len, tm):
    p = p_ref[...]
    rolled = pltpu.roll(p, 1, 0)
    row = lax.broadcasted_iota(jnp.int32, (tm, 1), 0)
    if seq_len >= tm:
        prev = jnp.where(row == 0, ext_ref[0:1, :], rolled)
    else:
        prev = jnp.where(row % seq_len == 0, ext_ref[...], rolled)
    pm = p + (prev - p) * mu_ref[...]
    r = pm[:, 0:A_WIDTH]
    k = pm[:, A_WIDTH:2 * A_WIDTH]
    v = pm[:, 2 * A_WIDTH:3 * A_WIDTH]
    tail = pm[:, 3 * A_WIDTH:A_COLS]
    w_pre = w0_ref[...] + _dot(jnp.tanh(tail).astype(BF16), wdec_ref[...])
    w_log = jnp.minimum(w_pre, 0.0) - jnp.log1p(jnp.exp(-jnp.abs(w_pre))) - 0.5
    decay = jnp.exp(-jnp.exp(w_log))
    a = _sigmoid(a0_ref[...] + _dot(tail.astype(BF16), wiclr_ref[...]))
    g = _dot(_sigmoid(tail).astype(BF16), wgate_ref[...])
    r_o[...] = r
    w_o[...] = decay
    k_o[...] = k
    a_o[...] = a
    v_o[...] = v
    g_o[...] = g
    for o_ref, val in zip(cm_outs, (r, decay, k, a, v)):
        o_ref[...] = val.T


def _rwkv_prep(pa, ext, seq_len, tm, wts, channel_major):
    t = pa.shape[0]
    full = lambda shape: pl.BlockSpec(shape, lambda i: (0,) * len(shape))
    ext_rows = SUBLANES if seq_len >= tm else tm
    ncm = 5 if channel_major else 0
    est = 2 * (tm * A_COLS * 4 * 2 + (6 + ncm) * tm * A_WIDTH * 4) + 12 * tm * A_WIDTH * 4
    out_specs = [pl.BlockSpec((tm, A_WIDTH), lambda i: (i, 0))] * 6
    out_shape = [jax.ShapeDtypeStruct((t, A_WIDTH), F32)] * 6
    if channel_major:
        nt = seq_len // tm
        out_specs += [pl.BlockSpec((A_WIDTH, tm), lambda i: (i // nt, i % nt))] * ncm
        out_shape += [jax.ShapeDtypeStruct((t // seq_len * A_WIDTH, seq_len), F32)] * ncm
    return pl.pallas_call(
        functools.partial(_rwkv_prep_kernel, seq_len=seq_len, tm=tm),
        grid=(t // tm,),
        in_specs=[pl.BlockSpec((tm, A_COLS), lambda i: (i, 0)),
                  pl.BlockSpec((ext_rows, A_COLS), lambda i: (i, 0)),
                  full((1, A_COLS)), full((1, A_WIDTH)), full((A_LORA_COLS, A_WIDTH)),
                  full((1, A_WIDTH)), full((A_LORA_COLS, A_WIDTH)), full((A_LORA_COLS, A_WIDTH))],
        out_specs=out_specs,
        out_shape=out_shape,
        compiler_params=_params(("arbitrary",), est),
        name="rwkv_prep",
    )(pa, ext, *wts)


def _rwkv_scan_kernel(r_in, w_in, k_in, a_in, v_in, kkp_ref, kap_ref, s0_ref, y_out, s_scr,
                      r_ref, w_ref, k_ref, kk_ref, b_ref, *cm_scr, tl, nv, dup, cm):
    tb = pl.program_id(1)
    nvg = nv // SUBLANES
    kdim = A_HEAD_DIM
    n = LANES // dup

    @pl.when(tb == 0)
    def _():
        s_scr[...] = s0_ref[...]

    if cm:
        v_scr, y_scr = cm_scr

        nseq = n // A_HEADS
        seq_rows = kdim * A_HEADS

        def tload(ref, chans):
            def first_row(c):
                off = c * A_HEADS
                return off if isinstance(c, int) else pl.multiple_of(off, A_HEADS)

            tiles = [ref[pl.ds(b * seq_rows + first_row(c), A_HEADS), :]
                     for c in chans for b in range(nseq)]
            return jnp.concatenate(tiles, axis=0).T

        def fill(c, ss):
            rows = pl.ds(pl.multiple_of(c * tl, tl), tl)
            r_ref[rows, :] = tload(r_in, [c] * dup)
            w_ref[rows, :] = tload(w_in, [c] * dup)
            kc = tload(k_in, [c] * dup)
            ac = tload(a_in, [c] * dup)
            kkc = kc * kkp_ref[pl.ds(c, 1), :]
            k_ref[rows, :] = kc * (1.0 + (ac - 1.0) * kap_ref[pl.ds(c, 1), :])
            kk_ref[rows, :] = kkc
            b_ref[rows, :] = ac
            return ss + kkc * kkc

        ss = lax.fori_loop(0, kdim, fill, jnp.zeros((tl, LANES), F32))
        denom = jnp.maximum(jnp.sqrt(ss), 1e-12)

        def normalise(c, carry):
            rows = pl.ds(pl.multiple_of(c * tl, tl), tl)
            kkn = kk_ref[rows, :] / denom
            kk_ref[rows, :] = kkn
            b_ref[rows, :] = -(kkn * b_ref[rows, :])
            return carry

        lax.fori_loop(0, kdim, normalise, 0)
        for r in range(nv):
            v_scr[pl.ds(r, tl, stride=nv), :] = tload(v_in, [vh * nv + r for vh in range(dup)])

        def vtile(t, vg):
            return v_scr[pl.ds(pl.multiple_of(t * nv, SUBLANES) + vg * SUBLANES, SUBLANES), :]

        def ystore(t, vg, val):
            y_scr[pl.ds(pl.multiple_of(t * nv, SUBLANES) + vg * SUBLANES, SUBLANES), :] = val
    else:
        def lanes(x):
            return jnp.concatenate([x] * dup, axis=-1) if dup > 1 else x

        def flat(x):
            return x.reshape(tl * kdim, LANES)

        k = lanes(k_in[...])
        a = lanes(a_in[...])
        kk = k * kkp_ref[...]
        kkn = kk / jnp.maximum(jnp.sqrt(jnp.sum(kk * kk, axis=1, keepdims=True)), 1e-12)
        r_ref[...] = flat(lanes(r_in[...]))
        w_ref[...] = flat(lanes(w_in[...]))
        k_ref[...] = flat(k * (1.0 + (a - 1.0) * kap_ref[...]))
        kk_ref[...] = flat(kkn)
        b_ref[...] = flat(-(kkn * a))

        def vtile(t, vg):
            return v_in[t, vg * SUBLANES:(vg + 1) * SUBLANES, :]

        def ystore(t, vg, val):
            y_out[t, vg * SUBLANES:(vg + 1) * SUBLANES, :] = val

    def bcast(ref, t, k):
        row = k * tl + t if cm else t * kdim + k
        return jnp.broadcast_to(ref[pl.ds(row, 1), :], (SUBLANES, LANES))

    zeros = lambda: [jnp.zeros((SUBLANES, LANES), F32) for _ in range(nvg)]

    sa0 = zeros()
    for k in range(kdim):
        kk_row = bcast(kk_ref, 0, k)
        for vg in range(nvg):
            sa0[vg] = sa0[vg] + s_scr[k, vg * SUBLANES:(vg + 1) * SUBLANES, :] * kk_row

    def step(t, sa):
        t_next = jnp.minimum(t + 1, tl - 1)
        vv = [vtile(t, vg) for vg in range(nvg)]
        y = zeros()
        sa_next = zeros()
        for k in range(kdim):
            w_row = bcast(w_ref, t, k)
            b_row = bcast(b_ref, t, k)
            k_row = bcast(k_ref, t, k)
            r_row = bcast(r_ref, t, k)
            kk_row = bcast(kk_ref, t_next, k)
            for vg in range(nvg):
                rows = slice(vg * SUBLANES, (vg + 1) * SUBLANES)
                s_new = s_scr[k, rows, :] * w_row + sa[vg] * b_row + vv[vg] * k_row
                s_scr[k, rows, :] = s_new
                y[vg] = y[vg] + s_new * r_row
                sa_next[vg] = sa_next[vg] + s_new * kk_row
        for vg in range(nvg):
            ystore(t, vg, y[vg])
        return tuple(sa_next)

    lax.fori_loop(0, tl, step, tuple(sa0))

    if cm:
        for r in range(nv):
            yt = y_scr[pl.ds(r, tl, stride=nv), :].T
            for vh in range(dup):
                for b in range(nseq):
                    y_out[pl.ds(b * seq_rows + (vh * nv + r) * A_HEADS, A_HEADS), :] = (
                        yt[vh * n + b * A_HEADS:vh * n + (b + 1) * A_HEADS, :])


def _rwkv_scan(ops, kkp, kap, s0, length, tl, nv, dup, cm):
    kdim = A_HEAD_DIM
    nl = s0.shape[-1]
    nbk = LANES // dup
    pspec = pl.BlockSpec((kdim, LANES), lambda g, tb: (0, g))
    sspec = pl.BlockSpec((kdim, nv, LANES), lambda g, tb: (0, 0, g))
    scratch = [pltpu.VMEM((tl * kdim, LANES), F32)] * 5
    if cm:
        assert nl == LANES and tl == LANES
        inspec = pl.BlockSpec((nbk * kdim, tl), lambda g, tb: (0, tb), pipeline_mode=pl.Buffered(1))
        in_specs = [inspec] * 5
        yspec = pl.BlockSpec((nbk * kdim, tl), lambda g, tb: (0, tb))
        yshape = jax.ShapeDtypeStruct((nbk * kdim, length), F32)
        scratch += [pltpu.VMEM((tl * nv, LANES), F32)] * 2
        est = 5 * nbk * kdim * tl * 4 + 2 * nbk * kdim * tl * 4 + 2 * tl * nv * LANES * 4
    else:
        kspec = pl.BlockSpec((tl, kdim, nbk), lambda g, tb: (tb, 0, g))
        yspec = pl.BlockSpec((tl, nv, LANES), lambda g, tb: (tb, 0, g))
        in_specs = [kspec] * 4 + [yspec]
        yshape = jax.ShapeDtypeStruct((length, nv, nl), F32)
        est = 2 * (4 * tl * kdim * LANES * 4 + 2 * tl * nv * LANES * 4) + 4 * tl * kdim * LANES * 4
    est += 5 * kdim * nv * LANES * 4 + 5 * tl * kdim * LANES * 4
    return pl.pallas_call(
        functools.partial(_rwkv_scan_kernel, tl=tl, nv=nv, dup=dup, cm=cm),
        grid=(nl // LANES, length // tl),
        in_specs=in_specs + [pspec, pspec,
                             pl.BlockSpec((kdim, nv, LANES), lambda g, tb: (0, 0, g),
                                          pipeline_mode=pl.Buffered(1))],
        out_specs=[yspec, sspec],
        out_shape=[yshape, jax.ShapeDtypeStruct((kdim, nv, nl), F32)],
        scratch_shapes=scratch,
        compiler_params=_params(("arbitrary", "arbitrary"), est),
        name="rwkv_scan",
    )(*ops, kkp, kap, s0)


def _rwkv_recurrence(ops, k_k, k_a, wkv0, bn, length, cm):
    n = bn * A_HEADS
    dup = max(1, LANES // n)
    nv = A_HEAD_DIM // dup
    assert nv % SUBLANES == 0 and (n * dup) % LANES == 0, (bn, n)

    def to_t(x):
        x = x.reshape(bn, length, A_HEAD_DIM, A_HEADS)
        return jnp.transpose(x, (1, 2, 0, 3)).reshape(length, A_HEAD_DIM, n)

    def param_t(p):
        p = jnp.broadcast_to(p.reshape(1, A_HEADS, A_HEAD_DIM), (bn, A_HEADS, A_HEAD_DIM))
        p = jnp.transpose(p, (2, 0, 1)).reshape(A_HEAD_DIM, n).astype(F32)
        return jnp.tile(p, (1, dup))

    def vpack(x):
        lead = x.shape[0]
        x = x.reshape(lead, dup, nv, n)
        return jnp.transpose(x, (0, 2, 1, 3)).reshape(lead, nv, dup * n)

    def vunpack(x):
        lead = x.shape[0]
        x = x.reshape(lead, nv, dup, n)
        return jnp.transpose(x, (0, 2, 1, 3)).reshape(lead, A_HEAD_DIM, n)

    s0 = jnp.transpose(wkv0.astype(F32), (3, 2, 0, 1)).reshape(A_HEAD_DIM, A_HEAD_DIM, n)
    s0 = vpack(s0)
    if cm:
        y, s_last = _rwkv_scan(ops, param_t(k_k), param_t(k_a), s0, length, LANES, nv, dup, True)
    else:
        r, w, k, a, v = ops
        ops_t = [to_t(x) for x in (r, w, k, a)] + [vpack(to_t(v))]
        tl = _tile(length, 32, 1)
        y_t, s_last = _rwkv_scan(ops_t, param_t(k_k), param_t(k_a), s0, length, tl, nv, dup, False)
        y = vunpack(y_t).reshape(length, A_HEAD_DIM, bn, A_HEADS)
        y = jnp.transpose(y, (2, 0, 1, 3)).reshape(bn * length, A_WIDTH)
    s_last = vunpack(s_last).reshape(A_HEAD_DIM, A_HEAD_DIM, bn, A_HEADS)
    s_last = jnp.transpose(s_last, (2, 3, 1, 0))
    return y, s_last


def _retention_kernel(p_ref, cos_ref, sin_ref, qdec_ref, kdec_ref, intra_ref, cdec_ref, s0_ref,
                      y_ref, sout_ref, s_scr, *, nsub, c, chained):
    ci = pl.program_id(1)
    rows = nsub * c

    if chained:
        @pl.when(ci == 0)
        def _():
            s_scr[...] = s0_ref[...]

    lane = lax.broadcasted_iota(jnp.int32, (rows, 2 * B_QK_WIDTH), 1)
    half = B_QK_DIM // 2
    qk = p_ref[:, 0:2 * B_QK_WIDTH]
    swapped = jnp.where((lane % B_QK_DIM) < half, pltpu.roll(qk, 2 * B_QK_WIDTH - half, 1),
                        pltpu.roll(qk, half, 1))
    rot = qk * cos_ref[...] + swapped * sin_ref[...]
    q = rot[:, :B_QK_WIDTH] * (B_QK_DIM ** -0.5)
    k = rot[:, B_QK_WIDTH:]
    qd = (q * qdec_ref[...]).astype(BF16)
    kd = (k * kdec_ref[...]).astype(BF16)
    qb = q.astype(BF16)
    kb = k.astype(BF16)
    for h in range(B_HEADS):
        qs = slice(h * B_QK_DIM, (h + 1) * B_QK_DIM)
        vs = slice(2 * B_QK_WIDTH + h * B_V_DIM, 2 * B_QK_WIDTH + (h + 1) * B_V_DIM)
        gs = slice(2 * B_QK_WIDTH + B_V_WIDTH + h * B_V_DIM,
                   2 * B_QK_WIDTH + B_V_WIDTH + (h + 1) * B_V_DIM)
        if chained:
            st = s_scr[0, h]
        for u in range(nsub):
            rs = slice(u * c, (u + 1) * c)
            if not chained:
                st = s0_ref[u, h]
            vh = p_ref[rs, vs].astype(BF16)
            gh = p_ref[rs, gs]
            scores = lax.dot_general(qb[rs, qs], kb[rs, qs], (((1,), (1,)), ((), ())),
                                     preferred_element_type=F32) * intra_ref[h]
            o = _dot(scores.astype(BF16), vh) + _dot(qd[rs, qs], st.astype(BF16))
            st = st * cdec_ref[h] + lax.dot_general(
                kd[rs, qs], vh, (((0,), (0,)), ((), ())), preferred_element_type=F32)
            o = o * lax.rsqrt(jnp.mean(o * o, -1, keepdims=True) + NORM_EPS)
            y_ref[rs, h * B_V_DIM:(h + 1) * B_V_DIM] = o * (gh * _sigmoid(gh))
            if not chained:
                sout_ref[u, h] = st
        if chained:
            s_scr[0, h] = st

    if chained:
        @pl.when(ci == pl.num_programs(1) - 1)
        def _():
            sout_ref[...] = s_scr[...]


def _retention(pb, s0, bn, length, pos0):
    c = math.gcd(length, RET_CHUNK)
    nc = length // c
    chained = nc > 1
    nsub = _tile(nc, 8, 1) if chained else _tile(bn, 8, 1)
    sb = 1 if chained else nsub
    nstep = nc // nsub if chained else 1
    half = B_QK_DIM // 2
    inv = ROPE_BASE ** (-jnp.arange(half, dtype=F32) / half)
    pos = (pos0 + jnp.arange(length)).astype(F32)
    ang = pos[:, None] * inv[None, :]
    cos, sin = jnp.cos(ang), jnp.sin(ang)
    cosf = jnp.tile(jnp.concatenate([cos, cos], -1), (1, 2 * B_HEADS))
    sinf = jnp.tile(jnp.concatenate([-sin, sin], -1), (1, 2 * B_HEADS))
    log_g = jnp.log1p(-jnp.exp2(-5.0 - jnp.arange(B_HEADS, dtype=F32)))
    idx = jnp.arange(c, dtype=F32)
    diff = idx[:, None] - idx[None, :]
    intra = jnp.where(diff >= 0, jnp.exp(jnp.maximum(diff, 0.0) * log_g[:, None, None]), 0.0)
    q_dec = jnp.exp((idx + 1.0)[:, None] * log_g[None, :])
    k_dec = jnp.exp((c - 1.0 - idx)[:, None] * log_g[None, :])
    c_dec = jnp.exp(c * log_g)
    rows = nsub * c
    qdec = jnp.tile(jnp.repeat(q_dec, B_QK_DIM, axis=1), (nsub, 1))
    kdec = jnp.tile(jnp.repeat(k_dec, B_QK_DIM, axis=1), (nsub, 1))
    cdec = jnp.broadcast_to(c_dec[:, None, None], (B_HEADS, 1, B_V_DIM))
    if not chained:
        cosf = jnp.tile(cosf, (nsub, 1))
        sinf = jnp.tile(sinf, (nsub, 1))

    full = lambda shape: pl.BlockSpec(shape, lambda i, j: (0,) * len(shape))
    tspec = pl.BlockSpec((rows, 2 * B_QK_WIDTH), lambda i, j: (j, 0))
    sspec = pl.BlockSpec((sb, B_HEADS, B_QK_DIM, B_V_DIM), lambda i, j: (i, 0, 0, 0))
    est = 2 * (rows * B_COLS * 4 + rows * B_V_WIDTH * 4 + 2 * sb * B_HEADS * B_QK_DIM * B_V_DIM * 4
               + 2 * rows * 2 * B_QK_WIDTH * 4) \
        + 3 * sb * B_HEADS * B_QK_DIM * B_V_DIM * 4 + 8 * rows * B_COLS * 4
    return pl.pallas_call(
        functools.partial(_retention_kernel, nsub=nsub, c=c, chained=chained),
        grid=(bn // sb, nstep),
        in_specs=[pl.BlockSpec((rows, B_COLS), lambda i, j: (i * nstep + j, 0)),
                  tspec, tspec,
                  full((rows, B_QK_WIDTH)), full((rows, B_QK_WIDTH)), full((B_HEADS, c, c)),
                  full((B_HEADS, 1, B_V_DIM)), sspec],
        out_specs=[pl.BlockSpec((rows, B_V_WIDTH), lambda i, j: (i * nstep + j, 0)), sspec],
        out_shape=[jax.ShapeDtypeStruct((bn * length, B_V_WIDTH), F32),
                   jax.ShapeDtypeStruct((bn, B_HEADS, B_QK_DIM, B_V_DIM), F32)],
        scratch_shapes=[pltpu.VMEM((sb, B_HEADS, B_QK_DIM, B_V_DIM), F32)],
        compiler_params=_params(("arbitrary", "arbitrary"), est),
        name="retention",
    )(pb, cosf, sinf, qdec, kdec, intra, cdec, s0.astype(F32))


def _mix_out_kernel(x_ref, y_ref, r_ref, k_ref, a_ref, v_ref, g_ref, yb_ref, ka_ref, rk_ref, lg_ref,
                    lb_ref, ones_ref, w_ref, o_ref, *, y_cm):
    ones = ones_ref[...]
    y = y_ref[...].T if y_cm else y_ref[...]
    inv_d = 1.0 / A_HEAD_DIM
    mean = _seg_sum(y, ones) * inv_d
    d = y - mean
    var = _seg_sum(d * d, ones) * inv_d
    yn = d * lax.rsqrt(var + A_GN_EPS) * lg_ref[...] + lb_ref[...]
    v = v_ref[...]
    kmod = k_ref[...] * (1.0 + (a_ref[...] - 1.0) * ka_ref[...])
    bonus = _seg_sum(r_ref[...] * kmod * rk_ref[...], ones) * v
    ya = ((yn + bonus) * g_ref[...]).astype(BF16)
    o_ref[...] = (x_ref[...] + _dot(ya, w_ref[0:A_WIDTH, :])
                  + _dot(yb_ref[...].astype(BF16), w_ref[A_WIDTH:, :]))


def _mix_out(x, y, r, k, a, v, g, yb, k_a, r_k, lnx_g, lnx_b, ones, w_out, seq_len, y_cm):
    t, d = x.shape
    tm = _tile(seq_len, 512, LANES) if y_cm else _tile(t, 512)
    row = lambda n: pl.BlockSpec((tm, n), lambda i: (i, 0))
    full = lambda shape: pl.BlockSpec(shape, lambda i: (0,) * len(shape))
    est = 2 * (2 * tm * d * 4 + 7 * tm * A_WIDTH * 4 + (A_WIDTH + B_V_WIDTH) * d * 2) + 10 * tm * A_WIDTH * 4
    vec = lambda p: p.reshape(1, A_WIDTH)
    nt = seq_len // tm if y_cm else 1
    yspec = pl.BlockSpec((A_WIDTH, tm), lambda i: (i // nt, i % nt)) if y_cm else row(A_WIDTH)
    return pl.pallas_call(
        functools.partial(_mix_out_kernel, y_cm=y_cm),
        grid=(t // tm,),
        in_specs=[row(d), yspec] + [row(A_WIDTH)] * 6 + [full((1, A_WIDTH))] * 4
                 + [full((A_WIDTH, A_WIDTH)), full((A_WIDTH + B_V_WIDTH, d))],
        out_specs=row(d),
        out_shape=jax.ShapeDtypeStruct((t, d), F32),
        compiler_params=_params(("arbitrary",), est),
        name="mix_out",
    )(x, y, r, k, a, v, g, yb, vec(k_a), vec(r_k), vec(lnx_g), vec(lnx_b), ones, w_out)


def _ffn_kernel(x_ref, g_ref, wg_ref, wu_ref, wd_ref, o_ref, h_scr):
    j = pl.program_id(1)

    @pl.when(j == 0)
    def _():
        x = x_ref[...]
        h_scr[...] = _rms(x, g_ref[...]).astype(BF16)
        o_ref[...] = x

    h = h_scr[...]
    a = _dot(h, wg_ref[...])
    b = _dot(h, wu_ref[...])
    m = (a * _sigmoid(a) * b).astype(BF16)
    o_ref[...] += _dot(m, wd_ref[...])


def _ffn(x, g, wg, wu, wd):
    t, d = x.shape
    ff = wg.shape[1]
    tm = _tile(t, 512)
    tf = _tile(ff, 1536, LANES)
    est = 2 * (2 * tm * d * 4 + 3 * d * tf * 2) + tm * d * 2 + 3 * tm * tf * 4
    return pl.pallas_call(
        _ffn_kernel,
        grid=(t // tm, ff // tf),
        in_specs=[pl.BlockSpec((tm, d), lambda i, j: (i, 0)),
                  pl.BlockSpec((1, d), lambda i, j: (0, 0)),
                  pl.BlockSpec((d, tf), lambda i, j: (0, j)),
                  pl.BlockSpec((d, tf), lambda i, j: (0, j)),
                  pl.BlockSpec((tf, d), lambda i, j: (j, 0))],
        out_specs=pl.BlockSpec((tm, d), lambda i, j: (i, 0)),
        out_shape=jax.ShapeDtypeStruct((t, d), F32),
        scratch_shapes=[pltpu.VMEM((tm, d), BF16)],
        compiler_params=_params(("arbitrary", "arbitrary"), est),
        name="ffn",
    )(x, g.reshape(1, d), wg, wu, wd)


def _rglru_kernel(gate_ref, xb_ref, tail0_ref, h0_ref, cw_ref, cb_ref, wa_ref, ba_ref, wx_ref,
                  bx_ref, lam_ref, y_ref, hlast_ref, tail_scr, h_scr, a_scr, b_scr, hs_scr, *, tl):
    tb = pl.program_id(1)

    @pl.when(tb == 0)
    def _():
        tail_scr[...] = tail0_ref[0]
        h_scr[...] = h0_ref[0]

    xb = xb_ref[...]
    full = jnp.concatenate([tail_scr[...], xb], axis=0)
    xc = cb_ref[...]
    for j in range(CONV_W):
        shift = CONV_W - 1 - j
        term = pltpu.roll(full, shift, 0) if shift else full
        xc = xc + term[SUBLANES:, :] * cw_ref[j:j + 1, :]
    tail_scr[...] = xb[tl - SUBLANES:, :]
    xcb = xc.astype(BF16)
    for n in range(C_BLOCKS):
        sl = slice(n * C_BLOCK_DIM, (n + 1) * C_BLOCK_DIM)
        xn = xcb[:, sl]
        r = _sigmoid(_dot(xn, wa_ref[n]) + ba_ref[:, sl])
        i = _sigmoid(_dot(xn, wx_ref[n]) + bx_ref[:, sl])
        lam = lam_ref[:, sl]
        softplus_neg_lam = jnp.maximum(-lam, 0.0) + jnp.log1p(jnp.exp(-jnp.abs(lam)))
        log_a = -LRU_C * r * softplus_neg_lam
        a = jnp.exp(log_a)
        gain = jnp.sqrt(-jnp.tanh(log_a) * (a * a + 1.0))
        a_scr[:, sl] = a
        b_scr[:, sl] = gain * i * xc[:, sl]

    sub = lax.broadcasted_iota(jnp.int32, (SUBLANES, D_RNN), 0)

    def tile(i, h):
        rows = pl.ds(pl.multiple_of(i * SUBLANES, SUBLANES), SUBLANES)
        a = a_scr[rows, :]
        b = b_scr[rows, :]
        for s in (1, 2, 4):
            a_prev = jnp.where(sub >= s, pltpu.roll(a, s, 0), 1.0)
            b_prev = jnp.where(sub >= s, pltpu.roll(b, s, 0), 0.0)
            b = b + a * b_prev
            a = a * a_prev
        hs = a * h + b
        hs_scr[rows, :] = hs
        return hs[SUBLANES - 1:SUBLANES, :]

    h = lax.fori_loop(0, tl // SUBLANES, tile, h_scr[...])
    h_scr[...] = h
    hlast_ref[0] = h
    gate = gate_ref[...]
    cdf = 0.5 * (1.0 + jnp.tanh(math.sqrt(2.0 / math.pi) * (gate + 0.044715 * (gate * gate * gate))))
    y_ref[...] = (gate * cdf * hs_scr[...]).astype(BF16)


def _rglru(gate, xb, tail0, h0, wts, bn, length):
    tl = _tile(length, 256)
    nt = length // tl
    full = lambda shape: pl.BlockSpec(shape, lambda i, j: (0,) * len(shape))
    row = pl.BlockSpec((tl, D_RNN), lambda i, j: (i * nt + j, 0))
    est = 2 * (3 * tl * D_RNN * 4 + 2 * C_BLOCKS * C_BLOCK_DIM * C_BLOCK_DIM * 2) + 10 * tl * D_RNN * 4
    return pl.pallas_call(
        functools.partial(_rglru_kernel, tl=tl),
        grid=(bn, nt),
        in_specs=[row, row,
                  pl.BlockSpec((1, SUBLANES, D_RNN), lambda i, j: (i, 0, 0)),
                  pl.BlockSpec((1, 1, D_RNN), lambda i, j: (i, 0, 0)),
                  full((CONV_W, D_RNN)), full((1, D_RNN)),
                  full((C_BLOCKS, C_BLOCK_DIM, C_BLOCK_DIM)), full((1, D_RNN)),
                  full((C_BLOCKS, C_BLOCK_DIM, C_BLOCK_DIM)), full((1, D_RNN)),
                  full((1, D_RNN))],
        out_specs=[row, pl.BlockSpec((1, 1, D_RNN), lambda i, j: (i, 0, 0))],
        out_shape=[jax.ShapeDtypeStruct((bn * length, D_RNN), BF16),
                   jax.ShapeDtypeStruct((bn, 1, D_RNN), F32)],
        scratch_shapes=[pltpu.VMEM((SUBLANES, D_RNN), F32), pltpu.VMEM((1, D_RNN), F32),
                        pltpu.VMEM((tl, D_RNN), F32), pltpu.VMEM((tl, D_RNN), F32),
                        pltpu.VMEM((tl, D_RNN), F32)],
        compiler_params=_params(("arbitrary", "arbitrary"), est),
        name="rglru",
    )(gate, xb, tail0, h0, *wts)


def _matmul_res_kernel(y_ref, w_ref, x_ref, o_ref):
    o_ref[...] = x_ref[...] + _dot(y_ref[...], w_ref[...])


def _matmul_res(y, w, x):
    t, d = x.shape
    kdim = y.shape[1]
    tm = _tile(t, 512, 16)
    est = 2 * (tm * kdim * 2 + kdim * d * 2 + 2 * tm * d * 4)
    return pl.pallas_call(
        _matmul_res_kernel,
        grid=(t // tm,),
        in_specs=[pl.BlockSpec((tm, kdim), lambda i: (i, 0)),
                  pl.BlockSpec((kdim, d), lambda i: (0, 0)),
                  pl.BlockSpec((tm, d), lambda i: (i, 0))],
        out_specs=pl.BlockSpec((tm, d), lambda i: (i, 0)),
        out_shape=jax.ShapeDtypeStruct((t, d), F32),
        compiler_params=_params(("arbitrary",), est),
        name="matmul_res",
    )(y, w, x)


def _router_kernel(x_ref, g_ref, wr_ref, h_ref, route_ref, *, tm):
    h = _rms(x_ref[...], g_ref[...])
    h_ref[...] = h
    logits = jnp.dot(h, wr_ref[...], preferred_element_type=F32, precision=lax.Precision.HIGHEST)
    lane = lax.broadcasted_iota(jnp.int32, (tm, LANES), 1).astype(F32)
    neg = jnp.float32(-jnp.inf)
    lg = jnp.where(lane < N_EXPERTS, logits, neg)
    m1 = jnp.max(lg, -1, keepdims=True)
    i1 = jnp.min(jnp.where(lg == m1, lane, float(LANES)), -1, keepdims=True)
    lg2 = jnp.where(lane == i1, neg, lg)
    m2 = jnp.max(lg2, -1, keepdims=True)
    i2 = jnp.min(jnp.where(lg2 == m2, lane, float(LANES)), -1, keepdims=True)
    e = jnp.exp(m2 - m1)
    g1 = 1.0 / (1.0 + e)
    g2 = e / (1.0 + e)
    route_ref[...] = jnp.where(lane == 0, i1, jnp.where(lane == 1, i2, jnp.where(
        lane == 2, g1, jnp.where(lane == 3, g2, 0.0))))


def _router(x, g, wr_pad):
    t, d = x.shape
    tm = _tile(t, 512)
    est = 2 * (2 * tm * d * 4 + d * LANES * 4 + tm * LANES * 4) + 2 * tm * d * 4
    return pl.pallas_call(
        functools.partial(_router_kernel, tm=tm),
        grid=(t // tm,),
        in_specs=[pl.BlockSpec((tm, d), lambda i: (i, 0)),
                  pl.BlockSpec((1, d), lambda i: (0, 0)),
                  pl.BlockSpec((d, LANES), lambda i: (0, 0))],
        out_specs=[pl.BlockSpec((tm, d), lambda i: (i, 0)),
                   pl.BlockSpec((tm, LANES), lambda i: (i, 0))],
        out_shape=[jax.ShapeDtypeStruct((t, d), F32), jax.ShapeDtypeStruct((t, LANES), F32)],
        compiler_params=_params(("arbitrary",), est),
        name="router",
    )(x, g.reshape(1, d), wr_pad)


def _moe_kernel(be_ref, bv_ref, tok_hbm, h_hbm, wg_ref, wu_ref, wd_ref, o_ref,
                idx_smem, xbuf, xbf, sem_idx, sem_rows, *, tm):
    i = pl.program_id(0)
    j = pl.program_id(1)
    nb = pl.num_programs(0)
    valid = bv_ref[i] != 0
    slot = i % 2
    nxt = jnp.minimum(i + 1, nb - 1)

    def idx_copy(blk, s):
        return pltpu.make_async_copy(tok_hbm.at[pl.ds(blk * tm, tm)],
                                     idx_smem.at[pl.ds(s * tm, tm)], sem_idx.at[s])

    def row_copy(tok, s, g, u):
        return pltpu.make_async_copy(h_hbm.at[pl.ds(tok, 1)], xbuf.at[s, g, pl.ds(u, 1)],
                                     sem_rows.at[s])

    def issue_rows(s):
        def issue(g, c):
            for u in range(SUBLANES):
                row_copy(idx_smem[s * tm + g * SUBLANES + u], s, g, u).start()
            return c
        lax.fori_loop(0, tm // SUBLANES, issue, 0)

    def drain_rows(s):
        def drain(g, c):
            for u in range(SUBLANES):
                row_copy(0, s, g, u).wait()
            return c
        lax.fori_loop(0, tm // SUBLANES, drain, 0)

    def block_start(s):
        ahead = idx_copy(nxt, 1 - s)
        ahead.start()
        drain_rows(s)
        xbf[...] = xbuf[s].reshape(xbf.shape).astype(BF16)
        ahead.wait()
        issue_rows(1 - s)

    @pl.when(jnp.logical_and(valid, j == 0))
    def _():
        @pl.when(i == 0)
        def _():
            first = idx_copy(0, 0)
            first.start()
            first.wait()
            issue_rows(0)

        for s in range(2):
            pl.when(slot == s)(functools.partial(block_start, s))

    @pl.when(valid)
    def _():
        x = xbf[...]
        a = _dot(x, wg_ref[...])
        b = _dot(x, wu_ref[...])
        m = (a * _sigmoid(a) * b).astype(BF16)
        contrib = _dot(m, wd_ref[...])

        @pl.when(j == 0)
        def _():
            o_ref[...] = contrib

        @pl.when(j > 0)
        def _():
            o_ref[...] += contrib

    is_last = jnp.logical_or(i == nb - 1, bv_ref[nxt] == 0)

    @pl.when(jnp.logical_and(jnp.logical_and(valid, is_last), j == pl.num_programs(1) - 1))
    def _():
        for s in range(2):
            pl.when(slot == s)(functools.partial(drain_rows, 1 - s))

    @pl.when(jnp.logical_and(jnp.logical_not(valid), j == 0))
    def _():
        o_ref[...] = jnp.zeros_like(o_ref)


def _moe(block_e, block_valid, tok_sorted, h, wg, wu, wd, tm):
    p = tok_sorted.shape[0]
    d = h.shape[1]
    ff = wg.shape[2]
    tf = _tile(ff, 1792, LANES)
    nf = ff // tf
    nb = p // tm

    def wcol(i, j, be, bv):
        return (be[i], 0, jnp.where(bv[i] != 0, j, nf - 1))

    def wrow(i, j, be, bv):
        return (be[i], jnp.where(bv[i] != 0, j, nf - 1), 0)

    est = 2 * (3 * d * tf * 2 + tm * d * 4) + 2 * tm * d * 4 + tm * d * 2 + 3 * tm * tf * 4
    grid_spec = pltpu.PrefetchScalarGridSpec(
        num_scalar_prefetch=2,
        grid=(nb, nf),
        in_specs=[pl.BlockSpec(memory_space=pl.ANY),
                  pl.BlockSpec(memory_space=pl.ANY),
                  pl.BlockSpec((None, d, tf), wcol),
                  pl.BlockSpec((None, d, tf), wcol),
                  pl.BlockSpec((None, tf, d), wrow)],
        out_specs=pl.BlockSpec((tm, d), lambda i, j, be, bv: (i, 0)),
        scratch_shapes=[pltpu.SMEM((2 * tm,), jnp.int32),
                        pltpu.VMEM((2, tm // SUBLANES, SUBLANES, d), F32),
                        pltpu.VMEM((tm, d), BF16), pltpu.SemaphoreType.DMA((2,)),
                        pltpu.SemaphoreType.DMA((2,))],
    )
    return pl.pallas_call(
        functools.partial(_moe_kernel, tm=tm),
        grid_spec=grid_spec,
        out_shape=jax.ShapeDtypeStruct((p, d), F32),
        compiler_params=_params(("arbitrary", "arbitrary"), est),
        name="moe",
    )(block_e, block_valid, tok_sorted, h, wg, wu, wd)


def _combine_kernel(pos_hbm, y_hbm, x_ref, route_ref, g_ref, o_ref, idx_smem, ybuf, sem_idx,
                    sem_rows, *, tm):
    i = pl.program_id(0)
    slot = i % 2
    nrows = 2 * tm

    def row_copy(src, s, g, u):
        return pltpu.make_async_copy(y_hbm.at[pl.ds(src, 1)], ybuf.at[s, g, pl.ds(u, 1)],
                                     sem_rows.at[s])

    def fetch(tile, s):
        idx_copy = pltpu.make_async_copy(pos_hbm.at[pl.ds(tile * nrows, nrows)],
                                         idx_smem.at[pl.ds(s * nrows, nrows)], sem_idx.at[s])
        idx_copy.start()
        idx_copy.wait()

        def issue(g, c):
            for u in range(SUBLANES):
                row_copy(idx_smem[s * nrows + g * SUBLANES + u], s, g, u).start()
            return c

        lax.fori_loop(0, nrows // SUBLANES, issue, 0)

    @pl.when(i == 0)
    def _():
        fetch(0, 0)

    def tile_body(s):
        @pl.when(i + 1 < pl.num_programs(0))
        def _():
            fetch(i + 1, 1 - s)

        def drain(g, c):
            for u in range(SUBLANES):
                row_copy(0, s, g, u).wait()
            return c

        lax.fori_loop(0, nrows // SUBLANES, drain, 0)
        route = route_ref[...]
        g1 = route[:, TOP_K:TOP_K + 1]
        g2 = route[:, TOP_K + 1:TOP_K + 2]
        nt = tm // SUBLANES
        y1 = ybuf[s, 0:nt].reshape(x_ref.shape)
        y2 = ybuf[s, nt:2 * nt].reshape(x_ref.shape)
        x = x_ref[...] + g1 * y1 + g2 * y2
        o_ref[...] = _rms(x, g_ref[...])

    for s in range(2):
        pl.when(slot == s)(functools.partial(tile_body, s))


def _combine(pos, y_sorted, x, route, g, tm):
    t, d = x.shape
    est = 2 * (2 * tm * d * 4 + tm * LANES * 4) + 4 * tm * d * 4 + 2 * tm * d * 4
    return pl.pallas_call(
        functools.partial(_combine_kernel, tm=tm),
        grid=(t // tm,),
        in_specs=[pl.BlockSpec(memory_space=pl.ANY),
                  pl.BlockSpec(memory_space=pl.ANY),
                  pl.BlockSpec((tm, d), lambda i: (i, 0)),
                  pl.BlockSpec((tm, LANES), lambda i: (i, 0)),
                  pl.BlockSpec((1, d), lambda i: (0, 0))],
        out_specs=pl.BlockSpec((tm, d), lambda i: (i, 0)),
        out_shape=jax.ShapeDtypeStruct((t, d), F32),
        scratch_shapes=[pltpu.SMEM((4 * tm,), jnp.int32),
                        pltpu.VMEM((2, 2 * tm // SUBLANES, SUBLANES, d), F32),
                        pltpu.SemaphoreType.DMA((2,)), pltpu.SemaphoreType.DMA((2,))],
        compiler_params=_params(("arbitrary",), est),
        name="combine",
    )(pos, y_sorted, x, route, g.reshape(1, d))


def _route_plan(route, tm):
    t = route.shape[0]
    flat_e = route[:, 0:TOP_K].astype(jnp.int32).reshape(-1)
    flat_tok = jnp.repeat(jnp.arange(t, dtype=jnp.int32), TOP_K)
    onehot = (flat_e[:, None] == jnp.arange(N_EXPERTS, dtype=jnp.int32)[None, :]).astype(jnp.int32)
    csum = jnp.cumsum(onehot, axis=0)
    counts = csum[-1]
    rank = jnp.sum((csum - onehot) * onehot, axis=1)
    padded = ((counts + tm - 1) // tm) * tm
    pend = jnp.cumsum(padded)
    pstart = pend - padded
    dest = pstart[flat_e] + rank
    nb = (t * TOP_K + tm - 1) // tm + N_EXPERTS
    p = nb * tm
    tok_sorted = jnp.zeros((p,), jnp.int32).at[dest].set(flat_tok)
    block_start = jnp.arange(nb, dtype=jnp.int32) * tm
    block_valid = (block_start < pend[-1]).astype(jnp.int32)
    last_e = jnp.sum((pend <= pend[-1] - 1).astype(jnp.int32))
    block_e = jnp.sum((pend[None, :] <= block_start[:, None]).astype(jnp.int32), axis=1)
    block_e = jnp.where(block_valid != 0, jnp.minimum(block_e, N_EXPERTS - 1), last_e)
    return tok_sorted, block_e.astype(jnp.int32), block_valid, dest.reshape(t, TOP_K)


def _tile_pos(dest, tm):
    t = dest.shape[0]
    return jnp.transpose(dest.reshape(t // tm, tm, TOP_K), (0, 2, 1)).reshape(-1)


def _rwkv_perm():
    j = np.arange(A_WIDTH)
    return (j % A_HEADS) * A_HEAD_DIM + j // A_HEADS


def _rwkv_cols():
    perm = _rwkv_perm()
    return np.concatenate([perm, A_WIDTH + perm, 2 * A_WIDTH + perm,
                           np.arange(3 * A_WIDTH, A_COLS)])


def _prep_weights(w):
    bf = lambda a: a.astype(BF16)
    perm = _rwkv_perm()
    cols = _rwkv_cols()
    head = np.arange(A_WIDTH) % A_HEADS
    lora = jnp.zeros((A_LORA_COLS, A_WIDTH), F32)
    out = dict(w)
    out['ones'] = jnp.asarray(head[:, None] == head[None, :], BF16)
    out['wdec_pad'] = bf(lora.at[0:64].set(w['a_w_decay'][0])[:, perm])
    out['wiclr_pad'] = bf(lora.at[64:128].set(w['a_w_iclr'][0])[:, perm])
    out['wgate_pad'] = bf(lora.at[128:256].set(w['a_w_gate'][0])[:, perm])
    out['mu_p'] = w['a_mu'][0][cols].reshape(1, -1)
    for name in ('a_w0', 'a_a0', 'a_k_a', 'a_r_k', 'a_lnx_g', 'a_lnx_b'):
        out[name + '_p'] = w[name][0].reshape(-1)[perm]
    for name in ('ffn_gate', 'ffn_up', 'ffn_down', 'w_in1', 'w_out1', 'c_w_a',
                 'c_w_x', 'moe_gate', 'moe_up', 'moe_down'):
        out[name] = bf(w[name][0])
    in_cols = np.concatenate([cols, np.arange(A_COLS, A_COLS + B_COLS)])
    out['w_in0'] = bf(w['w_in0'][0][:, in_cols])
    out_rows = np.concatenate([perm, np.arange(A_WIDTH, A_WIDTH + B_V_WIDTH)])
    out['w_out0'] = bf(w['w_out0'][0][out_rows])
    out['router_pad'] = jnp.zeros((w['moe_router'].shape[1], LANES), F32).at[:, :N_EXPERTS].set(
        w['moe_router'][0])
    return out


def _layer0(x, bn, length, pos0, shift, wkv, ret, w):
    t = bn * length
    cols = _rwkv_cols()
    pa, pb = _norm_matmul(x, w['norm_mix0'][0], w['w_in0'], (A_COLS, B_COLS))
    tm = _tile(t, 256) if length >= 256 else _tile(t, 256, length)
    shift = shift.astype(F32)[:, cols]
    if length >= tm:
        starts = jnp.arange(t // tm) * tm
        before = pa[jnp.maximum(starts - 1, 0)]
        first = jnp.where((starts % length == 0)[:, None], shift[starts // length], before)
        ext = jnp.zeros((t // tm, SUBLANES, A_COLS), F32).at[:, 0].set(first).reshape(-1, A_COLS)
    else:
        ext = jnp.repeat(shift, length, axis=0)
    row = lambda a: a.reshape(1, -1)
    cm = bn * A_HEADS <= LANES and length % LANES == 0 and tm % LANES == 0 and length >= tm
    outs = _rwkv_prep(
        pa, ext, length, tm,
        (w['mu_p'], row(w['a_w0_p']), w['wdec_pad'], row(w['a_a0_p']), w['wiclr_pad'],
         w['wgate_pad']), cm)
    r, dec, k, a, v, g = outs[:6]
    ops = outs[6:] if cm else (r, dec, k, a, v)
    y, wkv_new = _rwkv_recurrence(ops, w['a_k_k'][0], w['a_k_a'][0], wkv, bn, length, cm)
    yb, ret_new = _retention(pb, ret, bn, length, pos0)
    x = _mix_out(x, y, r, k, a, v, g, yb, w['a_k_a_p'], w['a_r_k_p'], w['a_lnx_g_p'],
                 w['a_lnx_b_p'], w['ones'], w['w_out0'], length, cm)
    x = _ffn(x, w['norm_ffn0'][0], w['ffn_gate'], w['ffn_up'], w['ffn_down'])
    shift_new = pa.reshape(bn, length, A_COLS)[:, -1][:, np.argsort(cols)]
    return x, shift_new, wkv_new, ret_new


def _layer1_mixer(x, bn, length, conv, hlru, w):
    assert length >= CONV_W - 1
    gate, xb = _norm_matmul(x, w['norm_mix1'][0], w['w_in1'], (D_RNN, D_RNN))
    tail0 = jnp.zeros((bn, SUBLANES, D_RNN), F32).at[:, SUBLANES - (CONV_W - 1):].set(conv.astype(F32))
    row = lambda a: a[0].reshape(1, -1)
    y, h_last = _rglru(gate, xb, tail0, hlru.astype(F32).reshape(bn, 1, D_RNN),
                       (w['c_conv_w'][0], row(w['c_conv_b']), w['c_w_a'], row(w['c_b_a']),
                        w['c_w_x'], row(w['c_b_x']), row(w['c_lambda'])), bn, length)
    x = _matmul_res(y, w['w_out1'], x)
    conv_new = xb.reshape(bn, length, D_RNN)[:, length - (CONV_W - 1):]
    return x, conv_new, h_last.reshape(bn, D_RNN)


def kernel(x_prompt, x_sample, state_rwkv_shift, state_rwkv_wkv, state_ret, state_lru_conv, state_lru_h, norm_mix0, w_in0, a_mu, a_w0, a_w_decay, a_a0, a_w_iclr, a_w_gate, a_k_k, a_k_a, a_r_k, a_lnx_g, a_lnx_b, w_out0, norm_ffn0, ffn_gate, ffn_up, ffn_down, norm_mix1, w_in1, c_conv_w, c_conv_b, c_w_a, c_b_a, c_w_x, c_b_x, c_lambda, w_out1, norm_ffn1, moe_router, moe_gate, moe_up, moe_down, norm_final):
    w = _prep_weights(dict(
        norm_mix0=norm_mix0, w_in0=w_in0, a_mu=a_mu, a_w0=a_w0, a_w_decay=a_w_decay, a_a0=a_a0,
        a_w_iclr=a_w_iclr, a_w_gate=a_w_gate, a_k_k=a_k_k, a_k_a=a_k_a, a_r_k=a_r_k, a_lnx_g=a_lnx_g,
        a_lnx_b=a_lnx_b, w_out0=w_out0, norm_ffn0=norm_ffn0, ffn_gate=ffn_gate, ffn_up=ffn_up,
        ffn_down=ffn_down, norm_mix1=norm_mix1, w_in1=w_in1, c_conv_w=c_conv_w, c_conv_b=c_conv_b,
        c_w_a=c_w_a, c_b_a=c_b_a, c_w_x=c_w_x, c_b_x=c_b_x, c_lambda=c_lambda, w_out1=w_out1,
        norm_ffn1=norm_ffn1, moe_router=moe_router, moe_gate=moe_gate, moe_up=moe_up,
        moe_down=moe_down))
    dt = x_prompt.dtype
    d = x_prompt.shape[-1]
    bp, lp = x_prompt.shape[:2]
    bs, ls = x_sample.shape[:2]
    zeros = lambda shape: jnp.zeros(shape, F32)
    groups = [
        (x_prompt.reshape(bp * lp, d), bp, lp, 0, zeros((bp, A_COLS)),
         zeros((bp, A_HEADS, A_HEAD_DIM, A_HEAD_DIM)), zeros((bp, B_HEADS, B_QK_DIM, B_V_DIM)),
         zeros((bp, CONV_W - 1, D_RNN)), zeros((bp, D_RNN))),
        (x_sample.reshape(bs * ls, d), bs, ls, PAST_LEN, state_rwkv_shift[0], state_rwkv_wkv[0],
         state_ret[0], state_lru_conv[0], state_lru_h[0]),
    ]
    xs, states, hs, routes = [], [], [], []
    for x, bn, length, pos0, shift, wkv, ret, conv, hlru in groups:
        x, s_shift, s_wkv, s_ret = _layer0(x, bn, length, pos0, shift, wkv, ret, w)
        x, s_conv, s_h = _layer1_mixer(x, bn, length, conv, hlru, w)
        h, route = _router(x, w['norm_ffn1'][0], w['router_pad'])
        xs.append(x)
        hs.append(h)
        routes.append(route)
        states.append((s_shift, s_wkv, s_ret, s_conv, s_h))
    h_all = jnp.concatenate(hs, 0)
    route_all = jnp.concatenate(routes, 0)
    tm_moe = 512 if h_all.shape[0] >= 4096 else 64
    tok_sorted, block_e, block_valid, dest = _route_plan(route_all, tm_moe)
    y_sorted = _moe(block_e, block_valid, tok_sorted, h_all, w['moe_gate'], w['moe_up'],
                    w['moe_down'], tm_moe)
    outs = []
    off = 0
    for x, route in zip(xs, routes):
        t = x.shape[0]
        tm = _tile(t, 256)
        pos = _tile_pos(dest[off:off + t], tm)
        outs.append(_combine(pos, y_sorted, x, route, norm_final, tm))
        off += t
    y_prompt = outs[0].reshape(bp, lp, d)
    y_sample = outs[1].reshape(bs, ls, d)
    st_p = tuple(s[None].astype(dt) for s in states[0])
    st_s = tuple(s[None].astype(dt) for s in states[1])
    return (y_prompt, y_sample) + st_p + st_s
```

```python
import functools
import math

import jax
import jax.numpy as jnp
import numpy as np
from jax import lax
from jax.experimental import pallas as pl
from jax.experimental.pallas import tpu as pltpu

F32 = jnp.float32
BF16 = jnp.bfloat16

A_HEADS = 8
A_HEAD_DIM = 64
A_WIDTH = A_HEADS * A_HEAD_DIM
A_LORA_COLS = 256
A_COLS = 3 * A_WIDTH + A_LORA_COLS
A_GN_EPS = A_HEAD_DIM * 1e-5
B_HEADS = 4
B_QK_DIM = 64
B_V_DIM = 128
B_QK_WIDTH = B_HEADS * B_QK_DIM
B_V_WIDTH = B_HEADS * B_V_DIM
B_COLS = 2 * B_QK_WIDTH + 2 * B_V_WIDTH
RET_CHUNK = 64
ROPE_BASE = 10000.0
D_RNN = 1280
C_BLOCKS = 10
C_BLOCK_DIM = D_RNN // C_BLOCKS
CONV_W = 4
LRU_C = 8.0
N_EXPERTS = 8
TOP_K = 2
NORM_EPS = 1e-6
PAST_LEN = 16384

LANES = 128
SUBLANES = 8
VMEM_PHYSICAL_BYTES = 64 * 1024 * 1024
VMEM_BUDGET_BYTES = VMEM_PHYSICAL_BYTES - 4 * 1024 * 1024


def _tile(n, pref, mult=SUBLANES):
    t = min(pref, n)
    while t > mult and (n % t or t % mult):
        t -= 1
    assert n % t == 0 and t % mult == 0, (n, pref, mult)
    return t


def _params(sem, est_bytes):
    limit = int(min(max(est_bytes * 5 // 4 + (4 << 20), 32 << 20), VMEM_BUDGET_BYTES))
    return pltpu.CompilerParams(dimension_semantics=sem, vmem_limit_bytes=limit)


def _rms(x, g):
    return x * lax.rsqrt(jnp.mean(x * x, -1, keepdims=True) + NORM_EPS) * g


def _dot(a, b):
    return jnp.dot(a, b, preferred_element_type=F32)


def _sigmoid(x):
    return 0.5 * (jnp.tanh(0.5 * x) + 1.0)


def _seg_sum(x, ones):
    hi = x.astype(BF16)
    lo = (x - hi.astype(F32)).astype(BF16)
    return _dot(hi, ones) + _dot(lo, ones)


def _norm_matmul_kernel(x_ref, g_ref, w_ref, *o_refs, splits):
    h = _rms(x_ref[...], g_ref[...]).astype(BF16)
    off = 0
    for o_ref, n in zip(o_refs, splits):
        o_ref[...] = _dot(h, w_ref[:, off:off + n])
        off += n


def _norm_matmul(x, g, w, splits):
    t, d = x.shape
    n = w.shape[1]
    tm = _tile(t, 512)
    est = 2 * (tm * d * 4 + d * n * 2 + tm * n * 4) + tm * n * 4
    return pl.pallas_call(
        functools.partial(_norm_matmul_kernel, splits=splits),
        grid=(t // tm,),
        in_specs=[pl.BlockSpec((tm, d), lambda i: (i, 0)),
                  pl.BlockSpec((1, d), lambda i: (0, 0)),
                  pl.BlockSpec((d, n), lambda i: (0, 0))],
        out_specs=[pl.BlockSpec((tm, s), lambda i: (i, 0)) for s in splits],
        out_shape=[jax.ShapeDtypeStruct((t, s), F32) for s in splits],
        compiler_params=_params(("arbitrary",), est),
        name="norm_matmul",
    )(x, g.reshape(1, d), w)


def _rwkv_prep_kernel(p_ref, ext_ref, mu_ref, w0_ref, wdec_ref, a0_ref, wiclr_ref, wgate_ref,
                      r_o, w_o, k_o, a_o, v_o, g_o, *cm_outs, seq_len, tm):
    p = p_ref[...]
    rolled = pltpu.roll(p, 1, 0)
    row = lax.broadcasted_iota(jnp.int32, (tm, 1), 0)
    if seq_len >= tm:
        prev = jnp.where(row == 0, ext_ref[0:1, :], rolled)
    else:
        prev = jnp.where(row % seq_len == 0, ext_ref[...], rolled)
    pm = p + (prev - p) * mu_ref[...]
    r = pm[:, 0:A_WIDTH]
    k = pm[:, A_WIDTH:2 * A_WIDTH]
    v = pm[:, 2 * A_WIDTH:3 * A_WIDTH]
    tail = pm[:, 3 * A_WIDTH:A_COLS]
    w_pre = w0_ref[...] + _dot(jnp.tanh(tail).astype(BF16), wdec_ref[...])
    w_log = jnp.minimum(w_pre, 0.0) - jnp.log1p(jnp.exp(-jnp.abs(w_pre))) - 0.5
    decay = jnp.exp(-jnp.exp(w_log))
    a = _sigmoid(a0_ref[...] + _dot(tail.astype(BF16), wiclr_ref[...]))
    g = _dot(_sigmoid(tail).astype(BF16), wgate_ref[...])
    r_o[...] = r
    w_o[...] = decay
    k_o[...] = k
    a_o[...] = a
    v_o[...] = v
    g_o[...] = g
    for o_ref, val in zip(cm_outs, (r, decay, k, a, v)):
        o_ref[...] = val.T


def _rwkv_prep(pa, ext, seq_len, tm, wts, channel_major):
    t = pa.shape[0]
    full = lambda shape: pl.BlockSpec(shape, lambda i: (0,) * len(shape))
    ext_rows = SUBLANES if seq_len >= tm else tm
    ncm = 5 if channel_major else 0
    est = 2 * (tm * A_COLS * 4 * 2 + (6 + ncm) * tm * A_WIDTH * 4) + 12 * tm * A_WIDTH * 4
    out_specs = [pl.BlockSpec((tm, A_WIDTH), lambda i: (i, 0))] * 6
    out_shape = [jax.ShapeDtypeStruct((t, A_WIDTH), F32)] * 6
    if channel_major:
        nt = seq_len // tm
        out_specs += [pl.BlockSpec((A_WIDTH, tm), lambda i: (i // nt, i % nt))] * ncm
        out_shape += [jax.ShapeDtypeStruct((t // seq_len * A_WIDTH, seq_len), F32)] * ncm
    return pl.pallas_call(
        functools.partial(_rwkv_prep_kernel, seq_len=seq_len, tm=tm),
        grid=(t // tm,),
        in_specs=[pl.BlockSpec((tm, A_COLS), lambda i: (i, 0)),
                  pl.BlockSpec((ext_rows, A_COLS), lambda i: (i, 0)),
                  full((1, A_COLS)), full((1, A_WIDTH)), full((A_LORA_COLS, A_WIDTH)),
                  full((1, A_WIDTH)), full((A_LORA_COLS, A_WIDTH)), full((A_LORA_COLS, A_WIDTH))],
        out_specs=out_specs,
        out_shape=out_shape,
        compiler_params=_params(("arbitrary",), est),
        name="rwkv_prep",
    )(pa, ext, *wts)


def _rwkv_scan_kernel(r_in, w_in, k_in, a_in, v_in, kkp_ref, kap_ref, s0_ref, y_out, s_scr,
                      r_ref, w_ref, k_ref, kk_ref, b_ref, *cm_scr, tl, nv, dup, cm):
    tb = pl.program_id(1)
    nvg = nv // SUBLANES
    kdim = A_HEAD_DIM
    n = LANES // dup

    @pl.when(tb == 0)
    def _():
        s_scr[...] = s0_ref[...]

    if cm:
        v_scr, y_scr = cm_scr

        nseq = n // A_HEADS
        seq_rows = kdim * A_HEADS

        def tload(ref, chans):
            def first_row(c):
                off = c * A_HEADS
                return off if isinstance(c, int) else pl.multiple_of(off, A_HEADS)

            tiles = [ref[pl.ds(b * seq_rows + first_row(c), A_HEADS), :]
                     for c in chans for b in range(nseq)]
            return jnp.concatenate(tiles, axis=0).T

        def fill(c, ss):
            rows = pl.ds(pl.multiple_of(c * tl, tl), tl)
            r_ref[rows, :] = tload(r_in, [c] * dup)
            w_ref[rows, :] = tload(w_in, [c] * dup)
            kc = tload(k_in, [c] * dup)
            ac = tload(a_in, [c] * dup)
            kkc = kc * kkp_ref[pl.ds(c, 1), :]
            k_ref[rows, :] = kc * (1.0 + (ac - 1.0) * kap_ref[pl.ds(c, 1), :])
            kk_ref[rows, :] = kkc
            b_ref[rows, :] = ac
            return ss + kkc * kkc

        ss = lax.fori_loop(0, kdim, fill, jnp.zeros((tl, LANES), F32))
        denom = jnp.maximum(jnp.sqrt(ss), 1e-12)

        def normalise(c, carry):
            rows = pl.ds(pl.multiple_of(c * tl, tl), tl)
            kkn = kk_ref[rows, :] / denom
            kk_ref[rows, :] = kkn
            b_ref[rows, :] = -(kkn * b_ref[rows, :])
            return carry

        lax.fori_loop(0, kdim, normalise, 0)
        for r in range(nv):
            v_scr[pl.ds(r, tl, stride=nv), :] = tload(v_in, [vh * nv + r for vh in range(dup)])

        def vtile(t, vg):
            return v_scr[pl.ds(pl.multiple_of(t * nv, SUBLANES) + vg * SUBLANES, SUBLANES), :]

        def ystore(t, vg, val):
            y_scr[pl.ds(pl.multiple_of(t * nv, SUBLANES) + vg * SUBLANES, SUBLANES), :] = val
    else:
        def lanes(x):
            return jnp.concatenate([x] * dup, axis=-1) if dup > 1 else x

        def flat(x):
            return x.reshape(tl * kdim, LANES)

        k = lanes(k_in[...])
        a = lanes(a_in[...])
        kk = k * kkp_ref[...]
        kkn = kk / jnp.maximum(jnp.sqrt(jnp.sum(kk * kk, axis=1, keepdims=True)), 1e-12)
        r_ref[...] = flat(lanes(r_in[...]))
        w_ref[...] = flat(lanes(w_in[...]))
        k_ref[...] = flat(k * (1.0 + (a - 1.0) * kap_ref[...]))
        kk_ref[...] = flat(kkn)
        b_ref[...] = flat(-(kkn * a))

        def vtile(t, vg):
            return v_in[t, vg * SUBLANES:(vg + 1) * SUBLANES, :]

        def ystore(t, vg, val):
            y_out[t, vg * SUBLANES:(vg + 1) * SUBLANES, :] = val

    def bcast(ref, t, k):
        row = k * tl + t if cm else t * kdim + k
        return jnp.broadcast_to(ref[pl.ds(row, 1), :], (SUBLANES, LANES))

    zeros = lambda: [jnp.zeros((SUBLANES, LANES), F32) for _ in range(nvg)]

    sa0 = zeros()
    for k in range(kdim):
        kk_row = bcast(kk_ref, 0, k)
        for vg in range(nvg):
            sa0[vg] = sa0[vg] + s_scr[k, vg * SUBLANES:(vg + 1) * SUBLANES, :] * kk_row

    def step(t, sa):
        t_next = jnp.minimum(t + 1, tl - 1)
        vv = [vtile(t, vg) for vg in range(nvg)]
        y = zeros()
        sa_next = zeros()
        for k in range(kdim):
            w_row = bcast(w_ref, t, k)
            b_row = bcast(b_ref, t, k)
            k_row = bcast(k_ref, t, k)
            r_row = bcast(r_ref, t, k)
            kk_row = bcast(kk_ref, t_next, k)
            for vg in range(nvg):
                rows = slice(vg * SUBLANES, (vg + 1) * SUBLANES)
                s_new = s_scr[k, rows, :] * w_row + sa[vg] * b_row + vv[vg] * k_row
                s_scr[k, rows, :] = s_new
                y[vg] = y[vg] + s_new * r_row
                sa_next[vg] = sa_next[vg] + s_new * kk_row
        for vg in range(nvg):
            ystore(t, vg, y[vg])
        return tuple(sa_next)

    lax.fori_loop(0, tl, step, tuple(sa0))

    if cm:
        for r in range(nv):
            yt = y_scr[pl.ds(r, tl, stride=nv), :].T
            for vh in range(dup):
                for b in range(nseq):
                    y_out[pl.ds(b * seq_rows + (vh * nv + r) * A_HEADS, A_HEADS), :] = (
                        yt[vh * n + b * A_HEADS:vh * n + (b + 1) * A_HEADS, :])


def _rwkv_scan(ops, kkp, kap, s0, length, tl, nv, dup, cm):
    kdim = A_HEAD_DIM
    nl = s0.shape[-1]
    nbk = LANES // dup
    pspec = pl.BlockSpec((kdim, LANES), lambda g, tb: (0, g))
    sspec = pl.BlockSpec((kdim, nv, LANES), lambda g, tb: (0, 0, g))
    scratch = [pltpu.VMEM((tl * kdim, LANES), F32)] * 5
    if cm:
        assert nl == LANES and tl == LANES
        inspec = pl.BlockSpec((nbk * kdim, tl), lambda g, tb: (0, tb), pipeline_mode=pl.Buffered(1))
        in_specs = [inspec] * 5
        yspec = pl.BlockSpec((nbk * kdim, tl), lambda g, tb: (0, tb))
        yshape = jax.ShapeDtypeStruct((nbk * kdim, length), F32)
        scratch += [pltpu.VMEM((tl * nv, LANES), F32)] * 2
        est = 5 * nbk * kdim * tl * 4 + 2 * nbk * kdim * tl * 4 + 2 * tl * nv * LANES * 4
    else:
        kspec = pl.BlockSpec((tl, kdim, nbk), lambda g, tb: (tb, 0, g))
        yspec = pl.BlockSpec((tl, nv, LANES), lambda g, tb: (tb, 0, g))
        in_specs = [kspec] * 4 + [yspec]
        yshape = jax.ShapeDtypeStruct((length, nv, nl), F32)
        est = 2 * (4 * tl * kdim * LANES * 4 + 2 * tl * nv * LANES * 4) + 4 * tl * kdim * LANES * 4
    est += 5 * kdim * nv * LANES * 4 + 5 * tl * kdim * LANES * 4
    return pl.pallas_call(
        functools.partial(_rwkv_scan_kernel, tl=tl, nv=nv, dup=dup, cm=cm),
        grid=(nl // LANES, length // tl),
        in_specs=in_specs + [pspec, pspec,
                             pl.BlockSpec((kdim, nv, LANES), lambda g, tb: (0, 0, g),
                                          pipeline_mode=pl.Buffered(1))],
        out_specs=[yspec, sspec],
        out_shape=[yshape, jax.ShapeDtypeStruct((kdim, nv, nl), F32)],
        scratch_shapes=scratch,
        compiler_params=_params(("arbitrary", "arbitrary"), est),
        name="rwkv_scan",
    )(*ops, kkp, kap, s0)


def _rwkv_recurrence(ops, k_k, k_a, wkv0, bn, length, cm):
    n = bn * A_HEADS
    dup = max(1, LANES // n)
    nv = A_HEAD_DIM // dup
    assert nv % SUBLANES == 0 and (n * dup) % LANES == 0, (bn, n)

    def to_t(x):
        x = x.reshape(bn, length * A_WIDTH).T
        return x.reshape(length, A_HEAD_DIM, n)

    def from_t(y):
        return y.reshape(length * A_WIDTH, bn).T.reshape(bn * length, A_WIDTH)

    def param_t(p):
        p = p.reshape(A_HEADS, A_HEAD_DIM).T.astype(F32)
        p = jnp.tile(p, (1, bn)) if cm else jnp.repeat(p, bn, axis=1)
        return jnp.tile(p, (1, dup))

    def vpack(x):
        lead = x.shape[0]
        x = x.reshape(lead, dup, nv, n)
        return jnp.transpose(x, (0, 2, 1, 3)).reshape(lead, nv, dup * n)

    def vunpack(x):
        lead = x.shape[0]
        x = x.reshape(lead, nv, dup, n)
        return jnp.transpose(x, (0, 2, 1, 3)).reshape(lead, A_HEAD_DIM, n)

    s_perm = (3, 2, 0, 1) if cm else (3, 2, 1, 0)
    s0 = jnp.transpose(wkv0.astype(F32), s_perm).reshape(A_HEAD_DIM, A_HEAD_DIM, n)
    s0 = vpack(s0)
    if cm:
        y, s_last = _rwkv_scan(ops, param_t(k_k), param_t(k_a), s0, length, LANES, nv, dup, True)
    else:
        r, w, k, a, v = ops
        ops_t = [to_t(x) for x in (r, w, k, a)] + [vpack(to_t(v))]
        tl = _tile(length, 32, 1)
        y_t, s_last = _rwkv_scan(ops_t, param_t(k_k), param_t(k_a), s0, length, tl, nv, dup, False)
        y = from_t(vunpack(y_t))
    if cm:
        s_last = vunpack(s_last).reshape(A_HEAD_DIM, A_HEAD_DIM, bn, A_HEADS)
        s_last = jnp.transpose(s_last, (2, 3, 1, 0))
    else:
        s_last = vunpack(s_last).reshape(A_HEAD_DIM, A_HEAD_DIM, A_HEADS, bn)
        s_last = jnp.transpose(s_last, (3, 2, 1, 0))
    return y, s_last


def _retention_kernel(p_ref, cos_ref, sin_ref, qdec_ref, kdec_ref, intra_ref, cdec_ref, s0_ref,
                      y_ref, sout_ref, s_scr, *, nsub, c, chained):
    ci = pl.program_id(1)
    rows = nsub * c

    if chained:
        @pl.when(ci == 0)
        def _():
            s_scr[...] = s0_ref[...]

    lane = lax.broadcasted_iota(jnp.int32, (rows, 2 * B_QK_WIDTH), 1)
    half = B_QK_DIM // 2
    qk = p_ref[:, 0:2 * B_QK_WIDTH]
    swapped = jnp.where((lane % B_QK_DIM) < half, pltpu.roll(qk, 2 * B_QK_WIDTH - half, 1),
                        pltpu.roll(qk, half, 1))
    rot = qk * cos_ref[...] + swapped * sin_ref[...]
    q = rot[:, :B_QK_WIDTH] * (B_QK_DIM ** -0.5)
    k = rot[:, B_QK_WIDTH:]
    qd = (q * qdec_ref[...]).astype(BF16)
    kd = (k * kdec_ref[...]).astype(BF16)
    qb = q.astype(BF16)
    kb = k.astype(BF16)
    for h in range(B_HEADS):
        qs = slice(h * B_QK_DIM, (h + 1) * B_QK_DIM)
        vs = slice(2 * B_QK_WIDTH + h * B_V_DIM, 2 * B_QK_WIDTH + (h + 1) * B_V_DIM)
        gs = slice(2 * B_QK_WIDTH + B_V_WIDTH + h * B_V_DIM,
                   2 * B_QK_WIDTH + B_V_WIDTH + (h + 1) * B_V_DIM)
        if chained:
            st = s_scr[0, h]
        for u in range(nsub):
            rs = slice(u * c, (u + 1) * c)
            if not chained:
                st = s0_ref[u, h]
            vh = p_ref[rs, vs].astype(BF16)
            gh = p_ref[rs, gs]
            scores = lax.dot_general(qb[rs, qs], kb[rs, qs], (((1,), (1,)), ((), ())),
                                     preferred_element_type=F32) * intra_ref[h]
            o = _dot(scores.astype(BF16), vh) + _dot(qd[rs, qs], st.astype(BF16))
            st = st * cdec_ref[h] + lax.dot_general(
                kd[rs, qs], vh, (((0,), (0,)), ((), ())), preferred_element_type=F32)
            o = o * lax.rsqrt(jnp.mean(o * o, -1, keepdims=True) + NORM_EPS)
            y_ref[rs, h * B_V_DIM:(h + 1) * B_V_DIM] = o * (gh * _sigmoid(gh))
            if not chained:
                sout_ref[u, h] = st
        if chained:
            s_scr[0, h] = st

    if chained:
        @pl.when(ci == pl.num_programs(1) - 1)
        def _():
            sout_ref[...] = s_scr[...]


def _retention(pb, s0, bn, length, pos0):
    c = math.gcd(length, RET_CHUNK)
    nc = length // c
    chained = nc > 1
    nsub = _tile(nc, 8, 1) if chained else _tile(bn, 8, 1)
    sb = 1 if chained else nsub
    nstep = nc // nsub if chained else 1
    half = B_QK_DIM // 2
    inv = ROPE_BASE ** (-jnp.arange(half, dtype=F32) / half)
    pos = (pos0 + jnp.arange(length)).astype(F32)
    ang = pos[:, None] * inv[None, :]
    cos, sin = jnp.cos(ang), jnp.sin(ang)
    cosf = jnp.tile(jnp.concatenate([cos, cos], -1), (1, 2 * B_HEADS))
    sinf = jnp.tile(jnp.concatenate([-sin, sin], -1), (1, 2 * B_HEADS))
    log_g = jnp.log1p(-jnp.exp2(-5.0 - jnp.arange(B_HEADS, dtype=F32)))
    idx = jnp.arange(c, dtype=F32)
    diff = idx[:, None] - idx[None, :]
    intra = jnp.where(diff >= 0, jnp.exp(jnp.maximum(diff, 0.0) * log_g[:, None, None]), 0.0)
    q_dec = jnp.exp((idx + 1.0)[:, None] * log_g[None, :])
    k_dec = jnp.exp((c - 1.0 - idx)[:, None] * log_g[None, :])
    c_dec = jnp.exp(c * log_g)
    rows = nsub * c
    qdec = jnp.tile(jnp.repeat(q_dec, B_QK_DIM, axis=1), (nsub, 1))
    kdec = jnp.tile(jnp.repeat(k_dec, B_QK_DIM, axis=1), (nsub, 1))
    cdec = jnp.broadcast_to(c_dec[:, None, None], (B_HEADS, 1, B_V_DIM))
    if not chained:
        cosf = jnp.tile(cosf, (nsub, 1))
        sinf = jnp.tile(sinf, (nsub, 1))

    full = lambda shape: pl.BlockSpec(shape, lambda i, j: (0,) * len(shape))
    tspec = pl.BlockSpec((rows, 2 * B_QK_WIDTH), lambda i, j: (j, 0))
    sspec = pl.BlockSpec((sb, B_HEADS, B_QK_DIM, B_V_DIM), lambda i, j: (i, 0, 0, 0))
    est = 2 * (rows * B_COLS * 4 + rows * B_V_WIDTH * 4 + 2 * sb * B_HEADS * B_QK_DIM * B_V_DIM * 4
               + 2 * rows * 2 * B_QK_WIDTH * 4) \
        + 3 * sb * B_HEADS * B_QK_DIM * B_V_DIM * 4 + 8 * rows * B_COLS * 4
    return pl.pallas_call(
        functools.partial(_retention_kernel, nsub=nsub, c=c, chained=chained),
        grid=(bn // sb, nstep),
        in_specs=[pl.BlockSpec((rows, B_COLS), lambda i, j: (i * nstep + j, 0)),
                  tspec, tspec,
                  full((rows, B_QK_WIDTH)), full((rows, B_QK_WIDTH)), full((B_HEADS, c, c)),
                  full((B_HEADS, 1, B_V_DIM)), sspec],
        out_specs=[pl.BlockSpec((rows, B_V_WIDTH), lambda i, j: (i * nstep + j, 0)), sspec],
        out_shape=[jax.ShapeDtypeStruct((bn * length, B_V_WIDTH), F32),
                   jax.ShapeDtypeStruct((bn, B_HEADS, B_QK_DIM, B_V_DIM), F32)],
        scratch_shapes=[pltpu.VMEM((sb, B_HEADS, B_QK_DIM, B_V_DIM), F32)],
        compiler_params=_params(("arbitrary", "arbitrary"), est),
        name="retention",
    )(pb, cosf, sinf, qdec, kdec, intra, cdec, s0.astype(F32))


def _mix_out_kernel(x_ref, y_ref, r_ref, k_ref, a_ref, v_ref, g_ref, yb_ref, ka_ref, rk_ref, lg_ref,
                    lb_ref, ones_ref, w_ref, o_ref, *, y_cm):
    ones = ones_ref[...]
    y = y_ref[...].T if y_cm else y_ref[...]
    inv_d = 1.0 / A_HEAD_DIM
    mean = _seg_sum(y, ones) * inv_d
    d = y - mean
    var = _seg_sum(d * d, ones) * inv_d
    yn = d * lax.rsqrt(var + A_GN_EPS) * lg_ref[...] + lb_ref[...]
    v = v_ref[...]
    kmod = k_ref[...] * (1.0 + (a_ref[...] - 1.0) * ka_ref[...])
    bonus = _seg_sum(r_ref[...] * kmod * rk_ref[...], ones) * v
    ya = ((yn + bonus) * g_ref[...]).astype(BF16)
    o_ref[...] = (x_ref[...] + _dot(ya, w_ref[0:A_WIDTH, :])
                  + _dot(yb_ref[...].astype(BF16), w_ref[A_WIDTH:, :]))


def _mix_out(x, y, r, k, a, v, g, yb, k_a, r_k, lnx_g, lnx_b, ones, w_out, seq_len, y_cm):
    t, d = x.shape
    tm = _tile(seq_len, 512, LANES) if y_cm else _tile(t, 512)
    row = lambda n: pl.BlockSpec((tm, n), lambda i: (i, 0))
    full = lambda shape: pl.BlockSpec(shape, lambda i: (0,) * len(shape))
    est = 2 * (2 * tm * d * 4 + 7 * tm * A_WIDTH * 4 + (A_WIDTH + B_V_WIDTH) * d * 2) + 10 * tm * A_WIDTH * 4
    vec = lambda p: p.reshape(1, A_WIDTH)
    nt = seq_len // tm if y_cm else 1
    yspec = pl.BlockSpec((A_WIDTH, tm), lambda i: (i // nt, i % nt)) if y_cm else row(A_WIDTH)
    return pl.pallas_call(
        functools.partial(_mix_out_kernel, y_cm=y_cm),
        grid=(t // tm,),
        in_specs=[row(d), yspec] + [row(A_WIDTH)] * 6 + [full((1, A_WIDTH))] * 4
                 + [full((A_WIDTH, A_WIDTH)), full((A_WIDTH + B_V_WIDTH, d))],
        out_specs=row(d),
        out_shape=jax.ShapeDtypeStruct((t, d), F32),
        compiler_params=_params(("arbitrary",), est),
        name="mix_out",
    )(x, y, r, k, a, v, g, yb, vec(k_a), vec(r_k), vec(lnx_g), vec(lnx_b), ones, w_out)


def _ffn_kernel(x_ref, g_ref, wg_ref, wu_ref, wd_ref, o_ref, h_scr):
    j = pl.program_id(1)

    @pl.when(j == 0)
    def _():
        x = x_ref[...]
        h_scr[...] = _rms(x, g_ref[...]).astype(BF16)
        o_ref[...] = x

    h = h_scr[...]
    a = _dot(h, wg_ref[...])
    b = _dot(h, wu_ref[...])
    m = (a * _sigmoid(a) * b).astype(BF16)
    o_ref[...] += _dot(m, wd_ref[...])


def _ffn(x, g, wg, wu, wd):
    t, d = x.shape
    ff = wg.shape[1]
    tm = _tile(t, 512)
    tf = _tile(ff, 1536, LANES)
    est = 2 * (2 * tm * d * 4 + 3 * d * tf * 2) + tm * d * 2 + 3 * tm * tf * 4
    return pl.pallas_call(
        _ffn_kernel,
        grid=(t // tm, ff // tf),
        in_specs=[pl.BlockSpec((tm, d), lambda i, j: (i, 0)),
                  pl.BlockSpec((1, d), lambda i, j: (0, 0)),
                  pl.BlockSpec((d, tf), lambda i, j: (0, j)),
                  pl.BlockSpec((d, tf), lambda i, j: (0, j)),
                  pl.BlockSpec((tf, d), lambda i, j: (j, 0))],
        out_specs=pl.BlockSpec((tm, d), lambda i, j: (i, 0)),
        out_shape=jax.ShapeDtypeStruct((t, d), F32),
        scratch_shapes=[pltpu.VMEM((tm, d), BF16)],
        compiler_params=_params(("arbitrary", "arbitrary"), est),
        name="ffn",
    )(x, g.reshape(1, d), wg, wu, wd)


def _rglru_kernel(gate_ref, xb_ref, tail0_ref, h0_ref, cw_ref, cb_ref, wa_ref, ba_ref, wx_ref,
                  bx_ref, lam_ref, y_ref, hlast_ref, tail_scr, h_scr, a_scr, b_scr, hs_scr, *, tl):
    tb = pl.program_id(1)

    @pl.when(tb == 0)
    def _():
        tail_scr[...] = tail0_ref[0]
        h_scr[...] = h0_ref[0]

    xb = xb_ref[...]
    full = jnp.concatenate([tail_scr[...], xb], axis=0)
    xc = cb_ref[...]
    for j in range(CONV_W):
        shift = CONV_W - 1 - j
        term = pltpu.roll(full, shift, 0) if shift else full
        xc = xc + term[SUBLANES:, :] * cw_ref[j:j + 1, :]
    tail_scr[...] = xb[tl - SUBLANES:, :]
    xcb = xc.astype(BF16)
    for n in range(C_BLOCKS):
        sl = slice(n * C_BLOCK_DIM, (n + 1) * C_BLOCK_DIM)
        xn = xcb[:, sl]
        r = _sigmoid(_dot(xn, wa_ref[n]) + ba_ref[:, sl])
        i = _sigmoid(_dot(xn, wx_ref[n]) + bx_ref[:, sl])
        lam = lam_ref[:, sl]
        softplus_neg_lam = jnp.maximum(-lam, 0.0) + jnp.log1p(jnp.exp(-jnp.abs(lam)))
        log_a = -LRU_C * r * softplus_neg_lam
        a = jnp.exp(log_a)
        gain = jnp.sqrt(-jnp.tanh(log_a) * (a * a + 1.0))
        a_scr[:, sl] = a
        b_scr[:, sl] = gain * i * xc[:, sl]

    sub = lax.broadcasted_iota(jnp.int32, (SUBLANES, D_RNN), 0)

    def tile(i, h):
        rows = pl.ds(pl.multiple_of(i * SUBLANES, SUBLANES), SUBLANES)
        a = a_scr[rows, :]
        b = b_scr[rows, :]
        for s in (1, 2, 4):
            a_prev = jnp.where(sub >= s, pltpu.roll(a, s, 0), 1.0)
            b_prev = jnp.where(sub >= s, pltpu.roll(b, s, 0), 0.0)
            b = b + a * b_prev
            a = a * a_prev
        hs = a * h + b
        hs_scr[rows, :] = hs
        return hs[SUBLANES - 1:SUBLANES, :]

    h = lax.fori_loop(0, tl // SUBLANES, tile, h_scr[...])
    h_scr[...] = h
    hlast_ref[0] = h
    gate = gate_ref[...]
    cdf = 0.5 * (1.0 + jnp.tanh(math.sqrt(2.0 / math.pi) * (gate + 0.044715 * (gate * gate * gate))))
    y_ref[...] = (gate * cdf * hs_scr[...]).astype(BF16)


def _rglru(gate, xb, tail0, h0, wts, bn, length):
    tl = _tile(length, 256)
    nt = length // tl
    full = lambda shape: pl.BlockSpec(shape, lambda i, j: (0,) * len(shape))
    row = pl.BlockSpec((tl, D_RNN), lambda i, j: (i * nt + j, 0))
    est = 2 * (3 * tl * D_RNN * 4 + 2 * C_BLOCKS * C_BLOCK_DIM * C_BLOCK_DIM * 2) + 10 * tl * D_RNN * 4
    return pl.pallas_call(
        functools.partial(_rglru_kernel, tl=tl),
        grid=(bn, nt),
        in_specs=[row, row,
                  pl.BlockSpec((1, SUBLANES, D_RNN), lambda i, j: (i, 0, 0)),
                  pl.BlockSpec((1, 1, D_RNN), lambda i, j: (i, 0, 0)),
                  full((CONV_W, D_RNN)), full((1, D_RNN)),
                  full((C_BLOCKS, C_BLOCK_DIM, C_BLOCK_DIM)), full((1, D_RNN)),
                  full((C_BLOCKS, C_BLOCK_DIM, C_BLOCK_DIM)), full((1, D_RNN)),
                  full((1, D_RNN))],
        out_specs=[row, pl.BlockSpec((1, 1, D_RNN), lambda i, j: (i, 0, 0))],
        out_shape=[jax.ShapeDtypeStruct((bn * length, D_RNN), BF16),
                   jax.ShapeDtypeStruct((bn, 1, D_RNN), F32)],
        scratch_shapes=[pltpu.VMEM((SUBLANES, D_RNN), F32), pltpu.VMEM((1, D_RNN), F32),
                        pltpu.VMEM((tl, D_RNN), F32), pltpu.VMEM((tl, D_RNN), F32),
                        pltpu.VMEM((tl, D_RNN), F32)],
        compiler_params=_params(("arbitrary", "arbitrary"), est),
        name="rglru",
    )(gate, xb, tail0, h0, *wts)


def _matmul_res_kernel(y_ref, w_ref, x_ref, o_ref):
    o_ref[...] = x_ref[...] + _dot(y_ref[...], w_ref[...])


def _matmul_res(y, w, x):
    t, d = x.shape
    kdim = y.shape[1]
    tm = _tile(t, 512, 16)
    est = 2 * (tm * kdim * 2 + kdim * d * 2 + 2 * tm * d * 4)
    return pl.pallas_call(
        _matmul_res_kernel,
        grid=(t // tm,),
        in_specs=[pl.BlockSpec((tm, kdim), lambda i: (i, 0)),
                  pl.BlockSpec((kdim, d), lambda i: (0, 0)),
                  pl.BlockSpec((tm, d), lambda i: (i, 0))],
        out_specs=pl.BlockSpec((tm, d), lambda i: (i, 0)),
        out_shape=jax.ShapeDtypeStruct((t, d), F32),
        compiler_params=_params(("arbitrary",), est),
        name="matmul_res",
    )(y, w, x)


def _router_kernel(x_ref, g_ref, wr_ref, h_ref, route_ref, *, tm):
    h = _rms(x_ref[...], g_ref[...])
    h_ref[...] = h
    logits = jnp.dot(h, wr_ref[...], preferred_element_type=F32, precision=lax.Precision.HIGHEST)
    lane = lax.broadcasted_iota(jnp.int32, (tm, LANES), 1).astype(F32)
    neg = jnp.float32(-jnp.inf)
    lg = jnp.where(lane < N_EXPERTS, logits, neg)
    m1 = jnp.max(lg, -1, keepdims=True)
    i1 = jnp.min(jnp.where(lg == m1, lane, float(LANES)), -1, keepdims=True)
    lg2 = jnp.where(lane == i1, neg, lg)
    m2 = jnp.max(lg2, -1, keepdims=True)
    i2 = jnp.min(jnp.where(lg2 == m2, lane, float(LANES)), -1, keepdims=True)
    e = jnp.exp(m2 - m1)
    g1 = 1.0 / (1.0 + e)
    g2 = e / (1.0 + e)
    route_ref[...] = jnp.where(lane == 0, i1, jnp.where(lane == 1, i2, jnp.where(
        lane == 2, g1, jnp.where(lane == 3, g2, 0.0))))


def _router(x, g, wr_pad):
    t, d = x.shape
    tm = _tile(t, 512)
    est = 2 * (2 * tm * d * 4 + d * LANES * 4 + tm * LANES * 4) + 2 * tm * d * 4
    return pl.pallas_call(
        functools.partial(_router_kernel, tm=tm),
        grid=(t // tm,),
        in_specs=[pl.BlockSpec((tm, d), lambda i: (i, 0)),
                  pl.BlockSpec((1, d), lambda i: (0, 0)),
                  pl.BlockSpec((d, LANES), lambda i: (0, 0))],
        out_specs=[pl.BlockSpec((tm, d), lambda i: (i, 0)),
                   pl.BlockSpec((tm, LANES), lambda i: (i, 0))],
        out_shape=[jax.ShapeDtypeStruct((t, d), F32), jax.ShapeDtypeStruct((t, LANES), F32)],
        compiler_params=_params(("arbitrary",), est),
        name="router",
    )(x, g.reshape(1, d), wr_pad)


def _moe_kernel(be_ref, bv_ref, tok_hbm, h_hbm, wg_ref, wu_ref, wd_ref, o_ref,
                idx_smem, xbuf, xbf, sem_idx, sem_rows, *, tm):
    i = pl.program_id(0)
    j = pl.program_id(1)
    nb = pl.num_programs(0)
    valid = bv_ref[i] != 0
    slot = i % 2
    nxt = jnp.minimum(i + 1, nb - 1)

    def idx_copy(blk, s):
        return pltpu.make_async_copy(tok_hbm.at[pl.ds(blk * tm, tm)],
                                     idx_smem.at[pl.ds(s * tm, tm)], sem_idx.at[s])

    def row_copy(tok, s, g, u):
        return pltpu.make_async_copy(h_hbm.at[pl.ds(tok, 1)], xbuf.at[s, g, pl.ds(u, 1)],
                                     sem_rows.at[s])

    def issue_rows(s):
        def issue(g, c):
            for u in range(SUBLANES):
                row_copy(idx_smem[s * tm + g * SUBLANES + u], s, g, u).start()
            return c
        lax.fori_loop(0, tm // SUBLANES, issue, 0)

    def drain_rows(s):
        def drain(g, c):
            for u in range(SUBLANES):
                row_copy(0, s, g, u).wait()
            return c
        lax.fori_loop(0, tm // SUBLANES, drain, 0)

    def block_start(s):
        ahead = idx_copy(nxt, 1 - s)
        ahead.start()
        drain_rows(s)
        xbf[...] = xbuf[s].reshape(xbf.shape).astype(BF16)
        ahead.wait()
        issue_rows(1 - s)

    @pl.when(jnp.logical_and(valid, j == 0))
    def _():
        @pl.when(i == 0)
        def _():
            first = idx_copy(0, 0)
            first.start()
            first.wait()
            issue_rows(0)

        for s in range(2):
            pl.when(slot == s)(functools.partial(block_start, s))

    @pl.when(valid)
    def _():
        x = xbf[...]
        a = _dot(x, wg_ref[...])
        b = _dot(x, wu_ref[...])
        m = (a * _sigmoid(a) * b).astype(BF16)
        contrib = _dot(m, wd_ref[...])

        @pl.when(j == 0)
        def _():
            o_ref[...] = contrib

        @pl.when(j > 0)
        def _():
            o_ref[...] += contrib

    is_last = jnp.logical_or(i == nb - 1, bv_ref[nxt] == 0)

    @pl.when(jnp.logical_and(jnp.logical_and(valid, is_last), j == pl.num_programs(1) - 1))
    def _():
        for s in range(2):
            pl.when(slot == s)(functools.partial(drain_rows, 1 - s))

    @pl.when(jnp.logical_and(jnp.logical_not(valid), j == 0))
    def _():
        o_ref[...] = jnp.zeros_like(o_ref)


def _moe(block_e, block_valid, tok_sorted, h, wg, wu, wd, tm):
    p = tok_sorted.shape[0]
    d = h.shape[1]
    ff = wg.shape[2]
    tf = _tile(ff, 1792, LANES)
    nf = ff // tf
    nb = p // tm

    def wcol(i, j, be, bv):
        return (be[i], 0, jnp.where(bv[i] != 0, j, nf - 1))

    def wrow(i, j, be, bv):
        return (be[i], jnp.where(bv[i] != 0, j, nf - 1), 0)

    est = 2 * (3 * d * tf * 2 + tm * d * 4) + 2 * tm * d * 4 + tm * d * 2 + 3 * tm * tf * 4
    grid_spec = pltpu.PrefetchScalarGridSpec(
        num_scalar_prefetch=2,
        grid=(nb, nf),
        in_specs=[pl.BlockSpec(memory_space=pl.ANY),
                  pl.BlockSpec(memory_space=pl.ANY),
                  pl.BlockSpec((None, d, tf), wcol),
                  pl.BlockSpec((None, d, tf), wcol),
                  pl.BlockSpec((None, tf, d), wrow)],
        out_specs=pl.BlockSpec((tm, d), lambda i, j, be, bv: (i, 0)),
        scratch_shapes=[pltpu.SMEM((2 * tm,), jnp.int32),
                        pltpu.VMEM((2, tm // SUBLANES, SUBLANES, d), F32),
                        pltpu.VMEM((tm, d), BF16), pltpu.SemaphoreType.DMA((2,)),
                        pltpu.SemaphoreType.DMA((2,))],
    )
    return pl.pallas_call(
        functools.partial(_moe_kernel, tm=tm),
        grid_spec=grid_spec,
        out_shape=jax.ShapeDtypeStruct((p, d), F32),
        compiler_params=_params(("arbitrary", "arbitrary"), est),
        name="moe",
    )(block_e, block_valid, tok_sorted, h, wg, wu, wd)


def _combine_kernel(pos_hbm, y_hbm, x_ref, route_ref, g_ref, o_ref, idx_smem, ybuf, sem_idx,
                    sem_rows, *, tm):
    i = pl.program_id(0)
    slot = i % 2
    nrows = 2 * tm

    def row_copy(src, s, g, u):
        return pltpu.make_async_copy(y_hbm.at[pl.ds(src, 1)], ybuf.at[s, g, pl.ds(u, 1)],
                                     sem_rows.at[s])

    def fetch(tile, s):
        idx_copy = pltpu.make_async_copy(pos_hbm.at[pl.ds(tile * nrows, nrows)],
                                         idx_smem.at[pl.ds(s * nrows, nrows)], sem_idx.at[s])
        idx_copy.start()
        idx_copy.wait()

        def issue(g, c):
            for u in range(SUBLANES):
                row_copy(idx_smem[s * nrows + g * SUBLANES + u], s, g, u).start()
            return c

        lax.fori_loop(0, nrows // SUBLANES, issue, 0)

    @pl.when(i == 0)
    def _():
        fetch(0, 0)

    def tile_body(s):
        @pl.when(i + 1 < pl.num_programs(0))
        def _():
            fetch(i + 1, 1 - s)

        def drain(g, c):
            for u in range(SUBLANES):
                row_copy(0, s, g, u).wait()
            return c

        lax.fori_loop(0, nrows // SUBLANES, drain, 0)
        route = route_ref[...]
        g1 = route[:, TOP_K:TOP_K + 1]
        g2 = route[:, TOP_K + 1:TOP_K + 2]
        nt = tm // SUBLANES
        y1 = ybuf[s, 0:nt].reshape(x_ref.shape)
        y2 = ybuf[s, nt:2 * nt].reshape(x_ref.shape)
        x = x_ref[...] + g1 * y1 + g2 * y2
        o_ref[...] = _rms(x, g_ref[...])

    for s in range(2):
        pl.when(slot == s)(functools.partial(tile_body, s))


def _combine(pos, y_sorted, x, route, g, tm):
    t, d = x.shape
    est = 2 * (2 * tm * d * 4 + tm * LANES * 4) + 4 * tm * d * 4 + 2 * tm * d * 4
    return pl.pallas_call(
        functools.partial(_combine_kernel, tm=tm),
        grid=(t // tm,),
        in_specs=[pl.BlockSpec(memory_space=pl.ANY),
                  pl.BlockSpec(memory_space=pl.ANY),
                  pl.BlockSpec((tm, d), lambda i: (i, 0)),
                  pl.BlockSpec((tm, LANES), lambda i: (i, 0)),
                  pl.BlockSpec((1, d), lambda i: (0, 0))],
        out_specs=pl.BlockSpec((tm, d), lambda i: (i, 0)),
        out_shape=jax.ShapeDtypeStruct((t, d), F32),
        scratch_shapes=[pltpu.SMEM((4 * tm,), jnp.int32),
                        pltpu.VMEM((2, 2 * tm // SUBLANES, SUBLANES, d), F32),
                        pltpu.SemaphoreType.DMA((2,)), pltpu.SemaphoreType.DMA((2,))],
        compiler_params=_params(("arbitrary",), est),
        name="combine",
    )(pos, y_sorted, x, route, g.reshape(1, d))


def _route_plan(route, tm):
    t = route.shape[0]
    flat_e = route[:, 0:TOP_K].astype(jnp.int32).reshape(-1)
    flat_tok = jnp.repeat(jnp.arange(t, dtype=jnp.int32), TOP_K)
    onehot = (flat_e[:, None] == jnp.arange(N_EXPERTS, dtype=jnp.int32)[None, :]).astype(jnp.int32)
    csum = jnp.cumsum(onehot, axis=0)
    counts = csum[-1]
    rank = jnp.sum((csum - onehot) * onehot, axis=1)
    padded = ((counts + tm - 1) // tm) * tm
    pend = jnp.cumsum(padded)
    pstart = pend - padded
    dest = pstart[flat_e] + rank
    nb = (t * TOP_K + tm - 1) // tm + N_EXPERTS
    p = nb * tm
    tok_sorted = jnp.zeros((p,), jnp.int32).at[dest].set(flat_tok, unique_indices=True)
    block_start = jnp.arange(nb, dtype=jnp.int32) * tm
    block_valid = (block_start < pend[-1]).astype(jnp.int32)
    last_e = jnp.sum((pend <= pend[-1] - 1).astype(jnp.int32))
    block_e = jnp.sum((pend[None, :] <= block_start[:, None]).astype(jnp.int32), axis=1)
    block_e = jnp.where(block_valid != 0, jnp.minimum(block_e, N_EXPERTS - 1), last_e)
    return tok_sorted, block_e.astype(jnp.int32), block_valid, dest.reshape(t, TOP_K)


def _tile_pos(dest, tm):
    t = dest.shape[0]
    return jnp.transpose(dest.reshape(t // tm, tm, TOP_K), (0, 2, 1)).reshape(-1)


def _rwkv_perm():
    j = np.arange(A_WIDTH)
    return (j % A_HEADS) * A_HEAD_DIM + j // A_HEADS


def _rwkv_cols():
    perm = _rwkv_perm()
    return np.concatenate([perm, A_WIDTH + perm, 2 * A_WIDTH + perm,
                           np.arange(3 * A_WIDTH, A_COLS)])


def _prep_weights(w):
    bf = lambda a: a.astype(BF16)
    perm = _rwkv_perm()
    cols = _rwkv_cols()
    head = np.arange(A_WIDTH) % A_HEADS
    lora = jnp.zeros((A_LORA_COLS, A_WIDTH), F32)
    out = dict(w)
    out['ones'] = jnp.asarray(head[:, None] == head[None, :], BF16)
    out['wdec_pad'] = bf(lora.at[0:64].set(w['a_w_decay'][0])[:, perm])
    out['wiclr_pad'] = bf(lora.at[64:128].set(w['a_w_iclr'][0])[:, perm])
    out['wgate_pad'] = bf(lora.at[128:256].set(w['a_w_gate'][0])[:, perm])
    out['mu_p'] = w['a_mu'][0][cols].reshape(1, -1)
    for name in ('a_w0', 'a_a0', 'a_k_a', 'a_r_k', 'a_lnx_g', 'a_lnx_b'):
        out[name + '_p'] = w[name][0].reshape(-1)[perm]
    for name in ('ffn_gate', 'ffn_up', 'ffn_down', 'w_in1', 'w_out1', 'c_w_a',
                 'c_w_x', 'moe_gate', 'moe_up', 'moe_down'):
        out[name] = bf(w[name][0])
    in_cols = np.concatenate([cols, np.arange(A_COLS, A_COLS + B_COLS)])
    out['w_in0'] = bf(w['w_in0'][0][:, in_cols])
    out_rows = np.concatenate([perm, np.arange(A_WIDTH, A_WIDTH + B_V_WIDTH)])
    out['w_out0'] = bf(w['w_out0'][0][out_rows])
    out['router_pad'] = jnp.zeros((w['moe_router'].shape[1], LANES), F32).at[:, :N_EXPERTS].set(
        w['moe_router'][0])
    return out


def _layer0(x, bn, length, pos0, shift, wkv, ret, w):
    t = bn * length
    cols = _rwkv_cols()
    pa, pb = _norm_matmul(x, w['norm_mix0'][0], w['w_in0'], (A_COLS, B_COLS))
    tm = _tile(t, 256) if length >= 256 else _tile(t, 256, length)
    shift = shift.astype(F32)[:, cols]
    if length >= tm:
        starts = jnp.arange(t // tm) * tm
        before = pa[jnp.maximum(starts - 1, 0)]
        first = jnp.where((starts % length == 0)[:, None], shift[starts // length], before)
        ext = jnp.zeros((t // tm, SUBLANES, A_COLS), F32).at[:, 0].set(first).reshape(-1, A_COLS)
    else:
        ext = jnp.repeat(shift, length, axis=0)
    row = lambda a: a.reshape(1, -1)
    cm = bn * A_HEADS <= LANES and length % LANES == 0 and tm % LANES == 0 and length >= tm
    outs = _rwkv_prep(
        pa, ext, length, tm,
        (w['mu_p'], row(w['a_w0_p']), w['wdec_pad'], row(w['a_a0_p']), w['wiclr_pad'],
         w['wgate_pad']), cm)
    r, dec, k, a, v, g = outs[:6]
    ops = outs[6:] if cm else (r, dec, k, a, v)
    y, wkv_new = _rwkv_recurrence(ops, w['a_k_k'][0], w['a_k_a'][0], wkv, bn, length, cm)
    yb, ret_new = _retention(pb, ret, bn, length, pos0)
    x = _mix_out(x, y, r, k, a, v, g, yb, w['a_k_a_p'], w['a_r_k_p'], w['a_lnx_g_p'],
                 w['a_lnx_b_p'], w['ones'], w['w_out0'], length, cm)
    x = _ffn(x, w['norm_ffn0'][0], w['ffn_gate'], w['ffn_up'], w['ffn_down'])
    shift_new = pa.reshape(bn, length, A_COLS)[:, -1][:, np.argsort(cols)]
    return x, shift_new, wkv_new, ret_new


def _layer1_mixer(x, bn, length, conv, hlru, w):
    assert length >= CONV_W - 1
    gate, xb = _norm_matmul(x, w['norm_mix1'][0], w['w_in1'], (D_RNN, D_RNN))
    tail0 = jnp.zeros((bn, SUBLANES, D_RNN), F32).at[:, SUBLANES - (CONV_W - 1):].set(conv.astype(F32))
    row = lambda a: a[0].reshape(1, -1)
    y, h_last = _rglru(gate, xb, tail0, hlru.astype(F32).reshape(bn, 1, D_RNN),
                       (w['c_conv_w'][0], row(w['c_conv_b']), w['c_w_a'], row(w['c_b_a']),
                        w['c_w_x'], row(w['c_b_x']), row(w['c_lambda'])), bn, length)
    x = _matmul_res(y, w['w_out1'], x)
    conv_new = xb.reshape(bn, length, D_RNN)[:, length - (CONV_W - 1):]
    return x, conv_new, h_last.reshape(bn, D_RNN)


def kernel(x_prompt, x_sample, state_rwkv_shift, state_rwkv_wkv, state_ret, state_lru_conv, state_lru_h, norm_mix0, w_in0, a_mu, a_w0, a_w_decay, a_a0, a_w_iclr, a_w_gate, a_k_k, a_k_a, a_r_k, a_lnx_g, a_lnx_b, w_out0, norm_ffn0, ffn_gate, ffn_up, ffn_down, norm_mix1, w_in1, c_conv_w, c_conv_b, c_w_a, c_b_a, c_w_x, c_b_x, c_lambda, w_out1, norm_ffn1, moe_router, moe_gate, moe_up, moe_down, norm_final):
    w = _prep_weights(dict(
        norm_mix0=norm_mix0, w_in0=w_in0, a_mu=a_mu, a_w0=a_w0, a_w_decay=a_w_decay, a_a0=a_a0,
        a_w_iclr=a_w_iclr, a_w_gate=a_w_gate, a_k_k=a_k_k, a_k_a=a_k_a, a_r_k=a_r_k, a_lnx_g=a_lnx_g,
        a_lnx_b=a_lnx_b, w_out0=w_out0, norm_ffn0=norm_ffn0, ffn_gate=ffn_gate, ffn_up=ffn_up,
        ffn_down=ffn_down, norm_mix1=norm_mix1, w_in1=w_in1, c_conv_w=c_conv_w, c_conv_b=c_conv_b,
        c_w_a=c_w_a, c_b_a=c_b_a, c_w_x=c_w_x, c_b_x=c_b_x, c_lambda=c_lambda, w_out1=w_out1,
        norm_ffn1=norm_ffn1, moe_router=moe_router, moe_gate=moe_gate, moe_up=moe_up,
        moe_down=moe_down))
    dt = x_prompt.dtype
    d = x_prompt.shape[-1]
    bp, lp = x_prompt.shape[:2]
    bs, ls = x_sample.shape[:2]
    zeros = lambda shape: jnp.zeros(shape, F32)
    groups = [
        (x_prompt.reshape(bp * lp, d), bp, lp, 0, zeros((bp, A_COLS)),
         zeros((bp, A_HEADS, A_HEAD_DIM, A_HEAD_DIM)), zeros((bp, B_HEADS, B_QK_DIM, B_V_DIM)),
         zeros((bp, CONV_W - 1, D_RNN)), zeros((bp, D_RNN))),
        (x_sample.reshape(bs * ls, d), bs, ls, PAST_LEN, state_rwkv_shift[0], state_rwkv_wkv[0],
         state_ret[0], state_lru_conv[0], state_lru_h[0]),
    ]
    xs, states, hs, routes = [], [], [], []
    for x, bn, length, pos0, shift, wkv, ret, conv, hlru in groups:
        x, s_shift, s_wkv, s_ret = _layer0(x, bn, length, pos0, shift, wkv, ret, w)
        x, s_conv, s_h = _layer1_mixer(x, bn, length, conv, hlru, w)
        h, route = _router(x, w['norm_ffn1'][0], w['router_pad'])
        xs.append(x)
        hs.append(h)
        routes.append(route)
        states.append((s_shift, s_wkv, s_ret, s_conv, s_h))
    h_all = jnp.concatenate(hs, 0)
    route_all = jnp.concatenate(routes, 0)
    tm_moe = 512 if h_all.shape[0] >= 4096 else 64
    tok_sorted, block_e, block_valid, dest = _route_plan(route_all, tm_moe)
    y_sorted = _moe(block_e, block_valid, tok_sorted, h_all, w['moe_gate'], w['moe_up'],
                    w['moe_down'], tm_moe)
    outs = []
    off = 0
    for x, route in zip(xs, routes):
        t = x.shape[0]
        tm = _tile(t, 256)
        pos = _tile_pos(dest[off:off + t], tm)
        outs.append(_combine(pos, y_sorted, x, route, norm_final, tm))
        off += t
    y_prompt = outs[0].reshape(bp, lp, d)
    y_sample = outs[1].reshape(bs, ls, d)
    st_p = tuple(s[None].astype(dt) for s in states[0])
    st_s = tuple(s[None].astype(dt) for s in states[1])
    return (y_prompt, y_sample) + st_p + st_s
```

```python
import functools
import math

import jax
import jax.numpy as jnp
import numpy as np
from jax import lax
from jax.experimental import pallas as pl
from jax.experimental.pallas import tpu as pltpu

F32 = jnp.float32
BF16 = jnp.bfloat16

A_HEADS = 8
A_HEAD_DIM = 64
A_WIDTH = A_HEADS * A_HEAD_DIM
A_LORA_COLS = 256
A_COLS = 3 * A_WIDTH + A_LORA_COLS
A_GN_EPS = A_HEAD_DIM * 1e-5
B_HEADS = 4
B_QK_DIM = 64
B_V_DIM = 128
B_QK_WIDTH = B_HEADS * B_QK_DIM
B_V_WIDTH = B_HEADS * B_V_DIM
B_COLS = 2 * B_QK_WIDTH + 2 * B_V_WIDTH
RET_CHUNK = 64
ROPE_BASE = 10000.0
D_RNN = 1280
C_BLOCKS = 10
C_BLOCK_DIM = D_RNN // C_BLOCKS
CONV_W = 4
LRU_C = 8.0
N_EXPERTS = 8
TOP_K = 2
NORM_EPS = 1e-6
PAST_LEN = 16384

LANES = 128
SUBLANES = 8
VMEM_PHYSICAL_BYTES = 64 * 1024 * 1024
VMEM_BUDGET_BYTES = VMEM_PHYSICAL_BYTES - 4 * 1024 * 1024


def _tile(n, pref, mult=SUBLANES):
    t = min(pref, n)
    while t > mult and (n % t or t % mult):
        t -= 1
    assert n % t == 0 and t % mult == 0, (n, pref, mult)
    return t


def _params(sem, est_bytes):
    limit = int(min(max(est_bytes * 5 // 4 + (4 << 20), 32 << 20), VMEM_BUDGET_BYTES))
    return pltpu.CompilerParams(dimension_semantics=sem, vmem_limit_bytes=limit)


def _rms(x, g):
    return x * lax.rsqrt(jnp.mean(x * x, -1, keepdims=True) + NORM_EPS) * g


def _dot(a, b):
    return jnp.dot(a, b, preferred_element_type=F32)


def _sigmoid(x):
    return 0.5 * (jnp.tanh(0.5 * x) + 1.0)


def _seg_sum(x, ones):
    hi = x.astype(BF16)
    lo = (x - hi.astype(F32)).astype(BF16)
    return _dot(hi, ones) + _dot(lo, ones)


def _norm_matmul_kernel(x_ref, g_ref, w_ref, *o_refs, splits):
    h = _rms(x_ref[...], g_ref[...]).astype(BF16)
    off = 0
    for o_ref, n in zip(o_refs, splits):
        o_ref[...] = _dot(h, w_ref[:, off:off + n])
        off += n


def _norm_matmul(x, g, w, splits):
    t, d = x.shape
    n = w.shape[1]
    tm = _tile(t, 512)
    est = 2 * (tm * d * 4 + d * n * 2 + tm * n * 4) + tm * n * 4
    return pl.pallas_call(
        functools.partial(_norm_matmul_kernel, splits=splits),
        grid=(t // tm,),
        in_specs=[pl.BlockSpec((tm, d), lambda i: (i, 0)),
                  pl.BlockSpec((1, d), lambda i: (0, 0)),
                  pl.BlockSpec((d, n), lambda i: (0, 0))],
        out_specs=[pl.BlockSpec((tm, s), lambda i: (i, 0)) for s in splits],
        out_shape=[jax.ShapeDtypeStruct((t, s), F32) for s in splits],
        compiler_params=_params(("arbitrary",), est),
        name="norm_matmul",
    )(x, g.reshape(1, d), w)


def _rwkv_prep_kernel(p_ref, ext_ref, mu_ref, w0_ref, wdec_ref, a0_ref, wiclr_ref, wgate_ref,
                      r_o, w_o, k_o, a_o, v_o, g_o, *cm_outs, seq_len, tm):
    p = p_ref[...]
    rolled = pltpu.roll(p, 1, 0)
    row = lax.broadcasted_iota(jnp.int32, (tm, 1), 0)
    if seq_len >= tm:
        prev = jnp.where(row == 0, ext_ref[0:1, :], rolled)
    else:
        prev = jnp.where(row % seq_len == 0, ext_ref[...], rolled)
    pm = p + (prev - p) * mu_ref[...]
    r = pm[:, 0:A_WIDTH]
    k = pm[:, A_WIDTH:2 * A_WIDTH]
    v = pm[:, 2 * A_WIDTH:3 * A_WIDTH]
    tail = pm[:, 3 * A_WIDTH:A_COLS]
    w_pre = w0_ref[...] + _dot(jnp.tanh(tail).astype(BF16), wdec_ref[...])
    w_log = jnp.minimum(w_pre, 0.0) - jnp.log1p(jnp.exp(-jnp.abs(w_pre))) - 0.5
    decay = jnp.exp(-jnp.exp(w_log))
    a = _sigmoid(a0_ref[...] + _dot(tail.astype(BF16), wiclr_ref[...]))
    g = _dot(_sigmoid(tail).astype(BF16), wgate_ref[...])
    r_o[...] = r
    w_o[...] = decay
    k_o[...] = k
    a_o[...] = a
    v_o[...] = v
    g_o[...] = g
    for o_ref, val in zip(cm_outs, (r, decay, k, a, v)):
        o_ref[...] = val.T


def _rwkv_prep(pa, ext, seq_len, tm, wts, channel_major):
    t = pa.shape[0]
    full = lambda shape: pl.BlockSpec(shape, lambda i: (0,) * len(shape))
    ext_rows = SUBLANES if seq_len >= tm else tm
    ncm = 5 if channel_major else 0
    est = 2 * (tm * A_COLS * 4 * 2 + (6 + ncm) * tm * A_WIDTH * 4) + 12 * tm * A_WIDTH * 4
    out_specs = [pl.BlockSpec((tm, A_WIDTH), lambda i: (i, 0))] * 6
    out_shape = [jax.ShapeDtypeStruct((t, A_WIDTH), F32)] * 6
    if channel_major:
        nt = seq_len // tm
        out_specs += [pl.BlockSpec((A_WIDTH, tm), lambda i: (i // nt, i % nt))] * ncm
        out_shape += [jax.ShapeDtypeStruct((t // seq_len * A_WIDTH, seq_len), F32)] * ncm
    return pl.pallas_call(
        functools.partial(_rwkv_prep_kernel, seq_len=seq_len, tm=tm),
        grid=(t // tm,),
        in_specs=[pl.BlockSpec((tm, A_COLS), lambda i: (i, 0)),
                  pl.BlockSpec((ext_rows, A_COLS), lambda i: (i, 0)),
                  full((1, A_COLS)), full((1, A_WIDTH)), full((A_LORA_COLS, A_WIDTH)),
                  full((1, A_WIDTH)), full((A_LORA_COLS, A_WIDTH)), full((A_LORA_COLS, A_WIDTH))],
        out_specs=out_specs,
        out_shape=out_shape,
        compiler_params=_params(("arbitrary",), est),
        name="rwkv_prep",
    )(pa, ext, *wts)


def _rwkv_scan_kernel(r_in, w_in, k_in, a_in, v_in, kkp_ref, kap_ref, s0_ref, y_out, s_scr,
                      r_ref, w_ref, k_ref, kk_ref, b_ref, *cm_scr, tl, nv, dup, cm):
    tb = pl.program_id(1)
    nvg = nv // SUBLANES
    kdim = A_HEAD_DIM
    n = LANES // dup

    @pl.when(tb == 0)
    def _():
        s_scr[...] = s0_ref[...]

    if cm:
        v_scr, y_scr = cm_scr

        nseq = n // A_HEADS
        seq_rows = kdim * A_HEADS

        def tload(ref, chans):
            def first_row(c):
                off = c * A_HEADS
                return off if isinstance(c, int) else pl.multiple_of(off, A_HEADS)

            tiles = [ref[pl.ds(b * seq_rows + first_row(c), A_HEADS), :]
                     for c in chans for b in range(nseq)]
            return jnp.concatenate(tiles, axis=0).T

        def fill(c, ss):
            rows = pl.ds(pl.multiple_of(c * tl, tl), tl)
            r_ref[rows, :] = tload(r_in, [c] * dup)
            w_ref[rows, :] = tload(w_in, [c] * dup)
            kc = tload(k_in, [c] * dup)
            ac = tload(a_in, [c] * dup)
            kkc = kc * kkp_ref[pl.ds(c, 1), :]
            k_ref[rows, :] = kc * (1.0 + (ac - 1.0) * kap_ref[pl.ds(c, 1), :])
            kk_ref[rows, :] = kkc
            b_ref[rows, :] = ac
            return ss + kkc * kkc

        ss = lax.fori_loop(0, kdim, fill, jnp.zeros((tl, LANES), F32))
        denom = jnp.maximum(jnp.sqrt(ss), 1e-12)

        def normalise(c, carry):
            rows = pl.ds(pl.multiple_of(c * tl, tl), tl)
            kkn = kk_ref[rows, :] / denom
            kk_ref[rows, :] = kkn
            b_ref[rows, :] = -(kkn * b_ref[rows, :])
            return carry

        lax.fori_loop(0, kdim, normalise, 0)
        for r in range(nv):
            v_scr[pl.ds(r, tl, stride=nv), :] = tload(v_in, [vh * nv + r for vh in range(dup)])

        def vtile(t, vg):
            return v_scr[pl.ds(pl.multiple_of(t * nv, SUBLANES) + vg * SUBLANES, SUBLANES), :]

        def ystore(t, vg, val):
            y_scr[pl.ds(pl.multiple_of(t * nv, SUBLANES) + vg * SUBLANES, SUBLANES), :] = val
    else:
        def lanes(x):
            return jnp.concatenate([x] * dup, axis=-1) if dup > 1 else x

        def flat(x):
            return x.reshape(tl * kdim, LANES)

        k = lanes(k_in[...])
        a = lanes(a_in[...])
        kk = k * kkp_ref[...]
        kkn = kk / jnp.maximum(jnp.sqrt(jnp.sum(kk * kk, axis=1, keepdims=True)), 1e-12)
        r_ref[...] = flat(lanes(r_in[...]))
        w_ref[...] = flat(lanes(w_in[...]))
        k_ref[...] = flat(k * (1.0 + (a - 1.0) * kap_ref[...]))
        kk_ref[...] = flat(kkn)
        b_ref[...] = flat(-(kkn * a))

        def vtile(t, vg):
            return v_in[t, vg * SUBLANES:(vg + 1) * SUBLANES, :]

        def ystore(t, vg, val):
            y_out[t, vg * SUBLANES:(vg + 1) * SUBLANES, :] = val

    def bcast(ref, t, k):
        row = k * tl + t if cm else t * kdim + k
        return jnp.broadcast_to(ref[pl.ds(row, 1), :], (SUBLANES, LANES))

    zeros = lambda: [jnp.zeros((SUBLANES, LANES), F32) for _ in range(nvg)]

    sa0 = zeros()
    for k in range(kdim):
        kk_row = bcast(kk_ref, 0, k)
        for vg in range(nvg):
            sa0[vg] = sa0[vg] + s_scr[k, vg * SUBLANES:(vg + 1) * SUBLANES, :] * kk_row

    def step(t, sa):
        t_next = jnp.minimum(t + 1, tl - 1)
        vv = [vtile(t, vg) for vg in range(nvg)]
        y = zeros()
        sa_next = zeros()
        for k in range(kdim):
            w_row = bcast(w_ref, t, k)
            b_row = bcast(b_ref, t, k)
            k_row = bcast(k_ref, t, k)
            r_row = bcast(r_ref, t, k)
            kk_row = bcast(kk_ref, t_next, k)
            for vg in range(nvg):
                rows = slice(vg * SUBLANES, (vg + 1) * SUBLANES)
                s_new = s_scr[k, rows, :] * w_row + sa[vg] * b_row + vv[vg] * k_row
                s_scr[k, rows, :] = s_new
                y[vg] = y[vg] + s_new * r_row
                sa_next[vg] = sa_next[vg] + s_new * kk_row
        for vg in range(nvg):
            ystore(t, vg, y[vg])
        return tuple(sa_next)

    lax.fori_loop(0, tl, step, tuple(sa0))

    if cm:
        for r in range(nv):
            yt = y_scr[pl.ds(r, tl, stride=nv), :].T
            for vh in range(dup):
                for b in range(nseq):
                    y_out[pl.ds(b * seq_rows + (vh * nv + r) * A_HEADS, A_HEADS), :] = (
                        yt[vh * n + b * A_HEADS:vh * n + (b + 1) * A_HEADS, :])


def _rwkv_scan(ops, kkp, kap, s0, length, tl, nv, dup, cm):
    kdim = A_HEAD_DIM
    nl = s0.shape[-1]
    nbk = LANES // dup
    pspec = pl.BlockSpec((kdim, LANES), lambda g, tb: (0, g))
    sspec = pl.BlockSpec((kdim, nv, LANES), lambda g, tb: (0, 0, g))
    scratch = [pltpu.VMEM((tl * kdim, LANES), F32)] * 5
    if cm:
        assert nl == LANES and tl == LANES
        inspec = pl.BlockSpec((nbk * kdim, tl), lambda g, tb: (0, tb), pipeline_mode=pl.Buffered(1))
        in_specs = [inspec] * 5
        yspec = pl.BlockSpec((nbk * kdim, tl), lambda g, tb: (0, tb))
        yshape = jax.ShapeDtypeStruct((nbk * kdim, length), F32)
        scratch += [pltpu.VMEM((tl * nv, LANES), F32)] * 2
        est = 5 * nbk * kdim * tl * 4 + 2 * nbk * kdim * tl * 4 + 2 * tl * nv * LANES * 4
    else:
        kspec = pl.BlockSpec((tl, kdim, nbk), lambda g, tb: (tb, 0, g))
        yspec = pl.BlockSpec((tl, nv, LANES), lambda g, tb: (tb, 0, g))
        in_specs = [kspec] * 4 + [yspec]
        yshape = jax.ShapeDtypeStruct((length, nv, nl), F32)
        est = 2 * (4 * tl * kdim * LANES * 4 + 2 * tl * nv * LANES * 4) + 4 * tl * kdim * LANES * 4
    est += 5 * kdim * nv * LANES * 4 + 5 * tl * kdim * LANES * 4
    return pl.pallas_call(
        functools.partial(_rwkv_scan_kernel, tl=tl, nv=nv, dup=dup, cm=cm),
        grid=(nl // LANES, length // tl),
        in_specs=in_specs + [pspec, pspec,
                             pl.BlockSpec((kdim, nv, LANES), lambda g, tb: (0, 0, g),
                                          pipeline_mode=pl.Buffered(1))],
        out_specs=[yspec, sspec],
        out_shape=[yshape, jax.ShapeDtypeStruct((kdim, nv, nl), F32)],
        scratch_shapes=scratch,
        compiler_params=_params(("arbitrary", "arbitrary"), est),
        name="rwkv_scan",
    )(*ops, kkp, kap, s0)


def _rwkv_recurrence(ops, k_k, k_a, wkv0, bn, length, cm):
    n = bn * A_HEADS
    dup = max(1, LANES // n)
    nv = A_HEAD_DIM // dup
    assert nv % SUBLANES == 0 and (n * dup) % LANES == 0, (bn, n)

    def to_t(x):
        x = x.reshape(bn, length * A_WIDTH).T
        return x.reshape(length, A_HEAD_DIM, n)

    def from_t(y):
        return y.reshape(length * A_WIDTH, bn).T.reshape(bn * length, A_WIDTH)

    def param_t(p):
        p = p.reshape(A_HEADS, A_HEAD_DIM).T.astype(F32)
        p = jnp.tile(p, (1, bn)) if cm else jnp.repeat(p, bn, axis=1)
        return jnp.tile(p, (1, dup))

    def vpack(x):
        lead = x.shape[0]
        x = x.reshape(lead, dup, nv, n)
        return jnp.transpose(x, (0, 2, 1, 3)).reshape(lead, nv, dup * n)

    def vunpack(x):
        lead = x.shape[0]
        x = x.reshape(lead, nv, dup, n)
        return jnp.transpose(x, (0, 2, 1, 3)).reshape(lead, A_HEAD_DIM, n)

    s_perm = (3, 2, 0, 1) if cm else (3, 2, 1, 0)
    s0 = jnp.transpose(wkv0.astype(F32), s_perm).reshape(A_HEAD_DIM, A_HEAD_DIM, n)
    s0 = vpack(s0)
    if cm:
        y, s_last = _rwkv_scan(ops, param_t(k_k), param_t(k_a), s0, length, LANES, nv, dup, True)
    else:
        r, w, k, a, v = ops
        ops_t = [to_t(x) for x in (r, w, k, a)] + [vpack(to_t(v))]
        tl = _tile(length, 32, 1)
        y_t, s_last = _rwkv_scan(ops_t, param_t(k_k), param_t(k_a), s0, length, tl, nv, dup, False)
        y = from_t(vunpack(y_t))
    if cm:
        s_last = vunpack(s_last).reshape(A_HEAD_DIM, A_HEAD_DIM, bn, A_HEADS)
        s_last = jnp.transpose(s_last, (2, 3, 1, 0))
    else:
        s_last = vunpack(s_last).reshape(A_HEAD_DIM, A_HEAD_DIM, A_HEADS, bn)
        s_last = jnp.transpose(s_last, (3, 2, 1, 0))
    return y, s_last


def _retention_kernel(p_ref, cos_ref, sin_ref, qdec_ref, kdec_ref, intra_ref, cdec_ref, s0_ref,
                      y_ref, sout_ref, s_scr, *, nsub, c, chained):
    ci = pl.program_id(1)
    rows = nsub * c

    if chained:
        @pl.when(ci == 0)
        def _():
            s_scr[...] = s0_ref[...]

    lane = lax.broadcasted_iota(jnp.int32, (rows, 2 * B_QK_WIDTH), 1)
    half = B_QK_DIM // 2
    qk = p_ref[:, 0:2 * B_QK_WIDTH]
    swapped = jnp.where((lane % B_QK_DIM) < half, pltpu.roll(qk, 2 * B_QK_WIDTH - half, 1),
                        pltpu.roll(qk, half, 1))
    rot = qk * cos_ref[...] + swapped * sin_ref[...]
    q = rot[:, :B_QK_WIDTH] * (B_QK_DIM ** -0.5)
    k = rot[:, B_QK_WIDTH:]
    qd = (q * qdec_ref[...]).astype(BF16)
    kd = (k * kdec_ref[...]).astype(BF16)
    qb = q.astype(BF16)
    kb = k.astype(BF16)
    for h in range(B_HEADS):
        qs = slice(h * B_QK_DIM, (h + 1) * B_QK_DIM)
        vs = slice(2 * B_QK_WIDTH + h * B_V_DIM, 2 * B_QK_WIDTH + (h + 1) * B_V_DIM)
        gs = slice(2 * B_QK_WIDTH + B_V_WIDTH + h * B_V_DIM,
                   2 * B_QK_WIDTH + B_V_WIDTH + (h + 1) * B_V_DIM)
        if chained:
            st = s_scr[0, h]
        for u in range(nsub):
            rs = slice(u * c, (u + 1) * c)
            if not chained:
                st = s0_ref[u, h]
            vh = p_ref[rs, vs].astype(BF16)
            gh = p_ref[rs, gs]
            scores = lax.dot_general(qb[rs, qs], kb[rs, qs], (((1,), (1,)), ((), ())),
                                     preferred_element_type=F32) * intra_ref[h]
            o = _dot(scores.astype(BF16), vh) + _dot(qd[rs, qs], st.astype(BF16))
            st = st * cdec_ref[h] + lax.dot_general(
                kd[rs, qs], vh, (((0,), (0,)), ((), ())), preferred_element_type=F32)
            o = o * lax.rsqrt(jnp.mean(o * o, -1, keepdims=True) + NORM_EPS)
            y_ref[rs, h * B_V_DIM:(h + 1) * B_V_DIM] = o * (gh * _sigmoid(gh))
            if not chained:
                sout_ref[u, h] = st
        if chained:
            s_scr[0, h] = st

    if chained:
        @pl.when(ci == pl.num_programs(1) - 1)
        def _():
            sout_ref[...] = s_scr[...]


def _retention(pb, s0, bn, length, pos0):
    c = math.gcd(length, RET_CHUNK)
    nc = length // c
    chained = nc > 1
    nsub = _tile(nc, 8, 1) if chained else _tile(bn, 8, 1)
    sb = 1 if chained else nsub
    nstep = nc // nsub if chained else 1
    half = B_QK_DIM // 2
    inv = ROPE_BASE ** (-jnp.arange(half, dtype=F32) / half)
    pos = (pos0 + jnp.arange(length)).astype(F32)
    ang = pos[:, None] * inv[None, :]
    cos, sin = jnp.cos(ang), jnp.sin(ang)
    cosf = jnp.tile(jnp.concatenate([cos, cos], -1), (1, 2 * B_HEADS))
    sinf = jnp.tile(jnp.concatenate([-sin, sin], -1), (1, 2 * B_HEADS))
    log_g = jnp.log1p(-jnp.exp2(-5.0 - jnp.arange(B_HEADS, dtype=F32)))
    idx = jnp.arange(c, dtype=F32)
    diff = idx[:, None] - idx[None, :]
    intra = jnp.where(diff >= 0, jnp.exp(jnp.maximum(diff, 0.0) * log_g[:, None, None]), 0.0)
    q_dec = jnp.exp((idx + 1.0)[:, None] * log_g[None, :])
    k_dec = jnp.exp((c - 1.0 - idx)[:, None] * log_g[None, :])
    c_dec = jnp.exp(c * log_g)
    rows = nsub * c
    qdec = jnp.tile(jnp.repeat(q_dec, B_QK_DIM, axis=1), (nsub, 1))
    kdec = jnp.tile(jnp.repeat(k_dec, B_QK_DIM, axis=1), (nsub, 1))
    cdec = jnp.broadcast_to(c_dec[:, None, None], (B_HEADS, 1, B_V_DIM))
    if not chained:
        cosf = jnp.tile(cosf, (nsub, 1))
        sinf = jnp.tile(sinf, (nsub, 1))

    full = lambda shape: pl.BlockSpec(shape, lambda i, j: (0,) * len(shape))
    tspec = pl.BlockSpec((rows, 2 * B_QK_WIDTH), lambda i, j: (j, 0))
    sspec = pl.BlockSpec((sb, B_HEADS, B_QK_DIM, B_V_DIM), lambda i, j: (i, 0, 0, 0))
    est = 2 * (rows * B_COLS * 4 + rows * B_V_WIDTH * 4 + 2 * sb * B_HEADS * B_QK_DIM * B_V_DIM * 4
               + 2 * rows * 2 * B_QK_WIDTH * 4) \
        + 3 * sb * B_HEADS * B_QK_DIM * B_V_DIM * 4 + 8 * rows * B_COLS * 4
    return pl.pallas_call(
        functools.partial(_retention_kernel, nsub=nsub, c=c, chained=chained),
        grid=(bn // sb, nstep),
        in_specs=[pl.BlockSpec((rows, B_COLS), lambda i, j: (i * nstep + j, 0)),
                  tspec, tspec,
                  full((rows, B_QK_WIDTH)), full((rows, B_QK_WIDTH)), full((B_HEADS, c, c)),
                  full((B_HEADS, 1, B_V_DIM)), sspec],
        out_specs=[pl.BlockSpec((rows, B_V_WIDTH), lambda i, j: (i * nstep + j, 0)), sspec],
        out_shape=[jax.ShapeDtypeStruct((bn * length, B_V_WIDTH), F32),
                   jax.ShapeDtypeStruct((bn, B_HEADS, B_QK_DIM, B_V_DIM), F32)],
        scratch_shapes=[pltpu.VMEM((sb, B_HEADS, B_QK_DIM, B_V_DIM), F32)],
        compiler_params=_params(("arbitrary", "arbitrary"), est),
        name="retention",
    )(pb, cosf, sinf, qdec, kdec, intra, cdec, s0.astype(F32))


def _mix_out_kernel(x_ref, y_ref, r_ref, k_ref, a_ref, v_ref, g_ref, yb_ref, ka_ref, rk_ref, lg_ref,
                    lb_ref, ones_ref, w_ref, o_ref, *, y_cm):
    ones = ones_ref[...]
    y = y_ref[...].T if y_cm else y_ref[...]
    inv_d = 1.0 / A_HEAD_DIM
    mean = _seg_sum(y, ones) * inv_d
    d = y - mean
    var = _seg_sum(d * d, ones) * inv_d
    yn = d * lax.rsqrt(var + A_GN_EPS) * lg_ref[...] + lb_ref[...]
    v = v_ref[...]
    kmod = k_ref[...] * (1.0 + (a_ref[...] - 1.0) * ka_ref[...])
    bonus = _seg_sum(r_ref[...] * kmod * rk_ref[...], ones) * v
    ya = ((yn + bonus) * g_ref[...]).astype(BF16)
    o_ref[...] = (x_ref[...] + _dot(ya, w_ref[0:A_WIDTH, :])
                  + _dot(yb_ref[...].astype(BF16), w_ref[A_WIDTH:, :]))


def _mix_out(x, y, r, k, a, v, g, yb, k_a, r_k, lnx_g, lnx_b, ones, w_out, seq_len, y_cm):
    t, d = x.shape
    tm = _tile(seq_len, 512, LANES) if y_cm else _tile(t, 512)
    row = lambda n: pl.BlockSpec((tm, n), lambda i: (i, 0))
    full = lambda shape: pl.BlockSpec(shape, lambda i: (0,) * len(shape))
    est = 2 * (2 * tm * d * 4 + 7 * tm * A_WIDTH * 4 + (A_WIDTH + B_V_WIDTH) * d * 2) + 10 * tm * A_WIDTH * 4
    vec = lambda p: p.reshape(1, A_WIDTH)
    nt = seq_len // tm if y_cm else 1
    yspec = pl.BlockSpec((A_WIDTH, tm), lambda i: (i // nt, i % nt)) if y_cm else row(A_WIDTH)
    return pl.pallas_call(
        functools.partial(_mix_out_kernel, y_cm=y_cm),
        grid=(t // tm,),
        in_specs=[row(d), yspec] + [row(A_WIDTH)] * 6 + [full((1, A_WIDTH))] * 4
                 + [full((A_WIDTH, A_WIDTH)), full((A_WIDTH + B_V_WIDTH, d))],
        out_specs=row(d),
        out_shape=jax.ShapeDtypeStruct((t, d), F32),
        compiler_params=_params(("arbitrary",), est),
        name="mix_out",
    )(x, y, r, k, a, v, g, yb, vec(k_a), vec(r_k), vec(lnx_g), vec(lnx_b), ones, w_out)


def _ffn_kernel(x_ref, g_ref, wg_ref, wu_ref, wd_ref, o_ref, h_scr):
    j = pl.program_id(1)

    @pl.when(j == 0)
    def _():
        x = x_ref[...]
        h_scr[...] = _rms(x, g_ref[...]).astype(BF16)
        o_ref[...] = x

    h = h_scr[...]
    a = _dot(h, wg_ref[...])
    b = _dot(h, wu_ref[...])
    m = (a * _sigmoid(a) * b).astype(BF16)
    o_ref[...] += _dot(m, wd_ref[...])


def _ffn(x, g, wg, wu, wd):
    t, d = x.shape
    ff = wg.shape[1]
    tm = _tile(t, 512)
    tf = _tile(ff, 1536, LANES)
    est = 2 * (2 * tm * d * 4 + 3 * d * tf * 2) + tm * d * 2 + 3 * tm * tf * 4
    return pl.pallas_call(
        _ffn_kernel,
        grid=(t // tm, ff // tf),
        in_specs=[pl.BlockSpec((tm, d), lambda i, j: (i, 0)),
                  pl.BlockSpec((1, d), lambda i, j: (0, 0)),
                  pl.BlockSpec((d, tf), lambda i, j: (0, j)),
                  pl.BlockSpec((d, tf), lambda i, j: (0, j)),
                  pl.BlockSpec((tf, d), lambda i, j: (j, 0))],
        out_specs=pl.BlockSpec((tm, d), lambda i, j: (i, 0)),
        out_shape=jax.ShapeDtypeStruct((t, d), F32),
        scratch_shapes=[pltpu.VMEM((tm, d), BF16)],
        compiler_params=_params(("arbitrary", "arbitrary"), est),
        name="ffn",
    )(x, g.reshape(1, d), wg, wu, wd)


def _rglru_kernel(gate_ref, xb_ref, tail0_ref, h0_ref, cw_ref, cb_ref, wa_ref, ba_ref, wx_ref,
                  bx_ref, lam_ref, y_ref, hlast_ref, tail_scr, h_scr, a_scr, b_scr, hs_scr, *, tl):
    tb = pl.program_id(1)

    @pl.when(tb == 0)
    def _():
        tail_scr[...] = tail0_ref[0]
        h_scr[...] = h0_ref[0]

    xb = xb_ref[...]
    full = jnp.concatenate([tail_scr[...], xb], axis=0)
    xc = cb_ref[...]
    for j in range(CONV_W):
        shift = CONV_W - 1 - j
        term = pltpu.roll(full, shift, 0) if shift else full
        xc = xc + term[SUBLANES:, :] * cw_ref[j:j + 1, :]
    tail_scr[...] = xb[tl - SUBLANES:, :]
    xcb = xc.astype(BF16)
    for n in range(C_BLOCKS):
        sl = slice(n * C_BLOCK_DIM, (n + 1) * C_BLOCK_DIM)
        xn = xcb[:, sl]
        r = _sigmoid(_dot(xn, wa_ref[n]) + ba_ref[:, sl])
        i = _sigmoid(_dot(xn, wx_ref[n]) + bx_ref[:, sl])
        lam = lam_ref[:, sl]
        softplus_neg_lam = jnp.maximum(-lam, 0.0) + jnp.log1p(jnp.exp(-jnp.abs(lam)))
        log_a = -LRU_C * r * softplus_neg_lam
        a = jnp.exp(log_a)
        gain = jnp.sqrt(-jnp.tanh(log_a) * (a * a + 1.0))
        a_scr[:, sl] = a
        b_scr[:, sl] = gain * i * xc[:, sl]

    def row(t, h):
        h = a_scr[pl.ds(t, 1), :] * h + b_scr[pl.ds(t, 1), :]
        hs_scr[pl.ds(t, 1), :] = h
        return h

    h = lax.fori_loop(0, tl, row, h_scr[...], unroll=8)
    h_scr[...] = h
    hlast_ref[0] = h
    gate = gate_ref[...]
    cdf = 0.5 * (1.0 + jnp.tanh(math.sqrt(2.0 / math.pi) * (gate + 0.044715 * (gate * gate * gate))))
    y_ref[...] = (gate * cdf * hs_scr[...]).astype(BF16)


def _rglru(gate, xb, tail0, h0, wts, bn, length):
    tl = _tile(length, 256)
    nt = length // tl
    full = lambda shape: pl.BlockSpec(shape, lambda i, j: (0,) * len(shape))
    row = pl.BlockSpec((tl, D_RNN), lambda i, j: (i * nt + j, 0))
    est = 2 * (3 * tl * D_RNN * 4 + 2 * C_BLOCKS * C_BLOCK_DIM * C_BLOCK_DIM * 2) + 10 * tl * D_RNN * 4
    return pl.pallas_call(
        functools.partial(_rglru_kernel, tl=tl),
        grid=(bn, nt),
        in_specs=[row, row,
                  pl.BlockSpec((1, SUBLANES, D_RNN), lambda i, j: (i, 0, 0)),
                  pl.BlockSpec((1, 1, D_RNN), lambda i, j: (i, 0, 0)),
                  full((CONV_W, D_RNN)), full((1, D_RNN)),
                  full((C_BLOCKS, C_BLOCK_DIM, C_BLOCK_DIM)), full((1, D_RNN)),
                  full((C_BLOCKS, C_BLOCK_DIM, C_BLOCK_DIM)), full((1, D_RNN)),
                  full((1, D_RNN))],
        out_specs=[row, pl.BlockSpec((1, 1, D_RNN), lambda i, j: (i, 0, 0))],
        out_shape=[jax.ShapeDtypeStruct((bn * length, D_RNN), BF16),
                   jax.ShapeDtypeStruct((bn, 1, D_RNN), F32)],
        scratch_shapes=[pltpu.VMEM((SUBLANES, D_RNN), F32), pltpu.VMEM((1, D_RNN), F32),
                        pltpu.VMEM((tl, D_RNN), F32), pltpu.VMEM((tl, D_RNN), F32),
                        pltpu.VMEM((tl, D_RNN), F32)],
        compiler_params=_params(("arbitrary", "arbitrary"), est),
        name="rglru",
    )(gate, xb, tail0, h0, *wts)


def _matmul_res_kernel(y_ref, w_ref, x_ref, o_ref):
    o_ref[...] = x_ref[...] + _dot(y_ref[...], w_ref[...])


def _matmul_res(y, w, x):
    t, d = x.shape
    kdim = y.shape[1]
    tm = _tile(t, 512, 16)
    est = 2 * (tm * kdim * 2 + kdim * d * 2 + 2 * tm * d * 4)
    return pl.pallas_call(
        _matmul_res_kernel,
        grid=(t // tm,),
        in_specs=[pl.BlockSpec((tm, kdim), lambda i: (i, 0)),
                  pl.BlockSpec((kdim, d), lambda i: (0, 0)),
                  pl.BlockSpec((tm, d), lambda i: (i, 0))],
        out_specs=pl.BlockSpec((tm, d), lambda i: (i, 0)),
        out_shape=jax.ShapeDtypeStruct((t, d), F32),
        compiler_params=_params(("arbitrary",), est),
        name="matmul_res",
    )(y, w, x)


def _router_kernel(x_ref, g_ref, wr_ref, h_ref, route_ref, *, tm):
    h = _rms(x_ref[...], g_ref[...])
    h_ref[...] = h
    logits = jnp.dot(h, wr_ref[...], preferred_element_type=F32, precision=lax.Precision.HIGHEST)
    lane = lax.broadcasted_iota(jnp.int32, (tm, LANES), 1).astype(F32)
    neg = jnp.float32(-jnp.inf)
    lg = jnp.where(lane < N_EXPERTS, logits, neg)
    m1 = jnp.max(lg, -1, keepdims=True)
    i1 = jnp.min(jnp.where(lg == m1, lane, float(LANES)), -1, keepdims=True)
    lg2 = jnp.where(lane == i1, neg, lg)
    m2 = jnp.max(lg2, -1, keepdims=True)
    i2 = jnp.min(jnp.where(lg2 == m2, lane, float(LANES)), -1, keepdims=True)
    e = jnp.exp(m2 - m1)
    g1 = 1.0 / (1.0 + e)
    g2 = e / (1.0 + e)
    route_ref[...] = jnp.where(lane == 0, i1, jnp.where(lane == 1, i2, jnp.where(
        lane == 2, g1, jnp.where(lane == 3, g2, 0.0))))


def _router(x, g, wr_pad):
    t, d = x.shape
    tm = _tile(t, 512)
    est = 2 * (2 * tm * d * 4 + d * LANES * 4 + tm * LANES * 4) + 2 * tm * d * 4
    return pl.pallas_call(
        functools.partial(_router_kernel, tm=tm),
        grid=(t // tm,),
        in_specs=[pl.BlockSpec((tm, d), lambda i: (i, 0)),
                  pl.BlockSpec((1, d), lambda i: (0, 0)),
                  pl.BlockSpec((d, LANES), lambda i: (0, 0))],
        out_specs=[pl.BlockSpec((tm, d), lambda i: (i, 0)),
                   pl.BlockSpec((tm, LANES), lambda i: (i, 0))],
        out_shape=[jax.ShapeDtypeStruct((t, d), F32), jax.ShapeDtypeStruct((t, LANES), F32)],
        compiler_params=_params(("arbitrary",), est),
        name="router",
    )(x, g.reshape(1, d), wr_pad)


def _moe_kernel(be_ref, bv_ref, tok_hbm, h_hbm, wg_ref, wu_ref, wd_ref, o_ref,
                idx_smem, xbuf, xbf, sem_idx, sem_rows, *, tm):
    i = pl.program_id(0)
    j = pl.program_id(1)
    nb = pl.num_programs(0)
    valid = bv_ref[i] != 0
    slot = i % 2
    nxt = jnp.minimum(i + 1, nb - 1)

    def idx_copy(blk, s):
        return pltpu.make_async_copy(tok_hbm.at[pl.ds(blk * tm, tm)],
                                     idx_smem.at[pl.ds(s * tm, tm)], sem_idx.at[s])

    def row_copy(tok, s, g, u):
        return pltpu.make_async_copy(h_hbm.at[pl.ds(tok, 1)], xbuf.at[s, g, pl.ds(u, 1)],
                                     sem_rows.at[s])

    def issue_rows(s):
        def issue(g, c):
            for u in range(SUBLANES):
                row_copy(idx_smem[s * tm + g * SUBLANES + u], s, g, u).start()
            return c
        lax.fori_loop(0, tm // SUBLANES, issue, 0)

    def drain_rows(s):
        def drain(g, c):
            for u in range(SUBLANES):
                row_copy(0, s, g, u).wait()
            return c
        lax.fori_loop(0, tm // SUBLANES, drain, 0)

    def block_start(s):
        ahead = idx_copy(nxt, 1 - s)
        ahead.start()
        drain_rows(s)
        xbf[...] = xbuf[s].reshape(xbf.shape).astype(BF16)
        ahead.wait()
        issue_rows(1 - s)

    @pl.when(jnp.logical_and(valid, j == 0))
    def _():
        @pl.when(i == 0)
        def _():
            first = idx_copy(0, 0)
            first.start()
            first.wait()
            issue_rows(0)

        for s in range(2):
            pl.when(slot == s)(functools.partial(block_start, s))

    @pl.when(valid)
    def _():
        x = xbf[...]
        a = _dot(x, wg_ref[...])
        b = _dot(x, wu_ref[...])
        m = (a * _sigmoid(a) * b).astype(BF16)
        contrib = _dot(m, wd_ref[...])

        @pl.when(j == 0)
        def _():
            o_ref[...] = contrib

        @pl.when(j > 0)
        def _():
            o_ref[...] += contrib

    is_last = jnp.logical_or(i == nb - 1, bv_ref[nxt] == 0)

    @pl.when(jnp.logical_and(jnp.logical_and(valid, is_last), j == pl.num_programs(1) - 1))
    def _():
        for s in range(2):
            pl.when(slot == s)(functools.partial(drain_rows, 1 - s))

    @pl.when(jnp.logical_and(jnp.logical_not(valid), j == 0))
    def _():
        o_ref[...] = jnp.zeros_like(o_ref)


def _moe(block_e, block_valid, tok_sorted, h, wg, wu, wd, tm):
    p = tok_sorted.shape[0]
    d = h.shape[1]
    ff = wg.shape[2]
    tf = _tile(ff, 1792, LANES)
    nf = ff // tf
    nb = p // tm

    def wcol(i, j, be, bv):
        return (be[i], 0, jnp.where(bv[i] != 0, j, nf - 1))

    def wrow(i, j, be, bv):
        return (be[i], jnp.where(bv[i] != 0, j, nf - 1), 0)

    est = 2 * (3 * d * tf * 2 + tm * d * 4) + 2 * tm * d * 4 + tm * d * 2 + 3 * tm * tf * 4
    grid_spec = pltpu.PrefetchScalarGridSpec(
        num_scalar_prefetch=2,
        grid=(nb, nf),
        in_specs=[pl.BlockSpec(memory_space=pl.ANY),
                  pl.BlockSpec(memory_space=pl.ANY),
                  pl.BlockSpec((None, d, tf), wcol),
                  pl.BlockSpec((None, d, tf), wcol),
                  pl.BlockSpec((None, tf, d), wrow)],
        out_specs=pl.BlockSpec((tm, d), lambda i, j, be, bv: (i, 0)),
        scratch_shapes=[pltpu.SMEM((2 * tm,), jnp.int32),
                        pltpu.VMEM((2, tm // SUBLANES, SUBLANES, d), F32),
                        pltpu.VMEM((tm, d), BF16), pltpu.SemaphoreType.DMA((2,)),
                        pltpu.SemaphoreType.DMA((2,))],
    )
    return pl.pallas_call(
        functools.partial(_moe_kernel, tm=tm),
        grid_spec=grid_spec,
        out_shape=jax.ShapeDtypeStruct((p, d), F32),
        compiler_params=_params(("arbitrary", "arbitrary"), est),
        name="moe",
    )(block_e, block_valid, tok_sorted, h, wg, wu, wd)


def _combine_kernel(pos_hbm, y_hbm, x_ref, route_ref, g_ref, o_ref, idx_smem, ybuf, sem_idx,
                    sem_rows, *, tm):
    i = pl.program_id(0)
    slot = i % 2
    nrows = 2 * tm

    def row_copy(src, s, g, u):
        return pltpu.make_async_copy(y_hbm.at[pl.ds(src, 1)], ybuf.at[s, g, pl.ds(u, 1)],
                                     sem_rows.at[s])

    def fetch(tile, s):
        idx_copy = pltpu.make_async_copy(pos_hbm.at[pl.ds(tile * nrows, nrows)],
                                         idx_smem.at[pl.ds(s * nrows, nrows)], sem_idx.at[s])
        idx_copy.start()
        idx_copy.wait()

        def issue(g, c):
            for u in range(SUBLANES):
                row_copy(idx_smem[s * nrows + g * SUBLANES + u], s, g, u).start()
            return c

        lax.fori_loop(0, nrows // SUBLANES, issue, 0)

    @pl.when(i == 0)
    def _():
        fetch(0, 0)

    def tile_body(s):
        @pl.when(i + 1 < pl.num_programs(0))
        def _():
            fetch(i + 1, 1 - s)

        def drain(g, c):
            for u in range(SUBLANES):
                row_copy(0, s, g, u).wait()
            return c

        lax.fori_loop(0, nrows // SUBLANES, drain, 0)
        route = route_ref[...]
        g1 = route[:, TOP_K:TOP_K + 1]
        g2 = route[:, TOP_K + 1:TOP_K + 2]
        nt = tm // SUBLANES
        y1 = ybuf[s, 0:nt].reshape(x_ref.shape)
        y2 = ybuf[s, nt:2 * nt].reshape(x_ref.shape)
        x = x_ref[...] + g1 * y1 + g2 * y2
        o_ref[...] = _rms(x, g_ref[...])

    for s in range(2):
        pl.when(slot == s)(functools.partial(tile_body, s))


def _combine(pos, y_sorted, x, route, g, tm):
    t, d = x.shape
    est = 2 * (2 * tm * d * 4 + tm * LANES * 4) + 4 * tm * d * 4 + 2 * tm * d * 4
    return pl.pallas_call(
        functools.partial(_combine_kernel, tm=tm),
        grid=(t // tm,),
        in_specs=[pl.BlockSpec(memory_space=pl.ANY),
                  pl.BlockSpec(memory_space=pl.ANY),
                  pl.BlockSpec((tm, d), lambda i: (i, 0)),
                  pl.BlockSpec((tm, LANES), lambda i: (i, 0)),
                  pl.BlockSpec((1, d), lambda i: (0, 0))],
        out_specs=pl.BlockSpec((tm, d), lambda i: (i, 0)),
        out_shape=jax.ShapeDtypeStruct((t, d), F32),
        scratch_shapes=[pltpu.SMEM((4 * tm,), jnp.int32),
                        pltpu.VMEM((2, 2 * tm // SUBLANES, SUBLANES, d), F32),
                        pltpu.SemaphoreType.DMA((2,)), pltpu.SemaphoreType.DMA((2,))],
        compiler_params=_params(("arbitrary",), est),
        name="combine",
    )(pos, y_sorted, x, route, g.reshape(1, d))


def _route_plan(route, tm):
    t = route.shape[0]
    flat_e = route[:, 0:TOP_K].astype(jnp.int32).reshape(-1)
    flat_tok = jnp.repeat(jnp.arange(t, dtype=jnp.int32), TOP_K)
    onehot = (flat_e[:, None] == jnp.arange(N_EXPERTS, dtype=jnp.int32)[None, :]).astype(jnp.int32)
    csum = jnp.cumsum(onehot, axis=0)
    counts = csum[-1]
    rank = jnp.sum((csum - onehot) * onehot, axis=1)
    padded = ((counts + tm - 1) // tm) * tm
    pend = jnp.cumsum(padded)
    pstart = pend - padded
    dest = pstart[flat_e] + rank
    nb = (t * TOP_K + tm - 1) // tm + N_EXPERTS
    p = nb * tm
    tok_sorted = jnp.zeros((p,), jnp.int32).at[dest].set(flat_tok, unique_indices=True)
    block_start = jnp.arange(nb, dtype=jnp.int32) * tm
    block_valid = (block_start < pend[-1]).astype(jnp.int32)
    last_e = jnp.sum((pend <= pend[-1] - 1).astype(jnp.int32))
    block_e = jnp.sum((pend[None, :] <= block_start[:, None]).astype(jnp.int32), axis=1)
    block_e = jnp.where(block_valid != 0, jnp.minimum(block_e, N_EXPERTS - 1), last_e)
    return tok_sorted, block_e.astype(jnp.int32), block_valid, dest.reshape(t, TOP_K)


def _tile_pos(dest, tm):
    t = dest.shape[0]
    return jnp.transpose(dest.reshape(t // tm, tm, TOP_K), (0, 2, 1)).reshape(-1)


def _rwkv_perm():
    j = np.arange(A_WIDTH)
    return (j % A_HEADS) * A_HEAD_DIM + j // A_HEADS


def _rwkv_cols():
    perm = _rwkv_perm()
    return np.concatenate([perm, A_WIDTH + perm, 2 * A_WIDTH + perm,
                           np.arange(3 * A_WIDTH, A_COLS)])


def _prep_weights(w):
    bf = lambda a: a.astype(BF16)
    perm = _rwkv_perm()
    cols = _rwkv_cols()
    head = np.arange(A_WIDTH) % A_HEADS
    lora = jnp.zeros((A_LORA_COLS, A_WIDTH), F32)
    out = dict(w)
    out['ones'] = jnp.asarray(head[:, None] == head[None, :], BF16)
    out['wdec_pad'] = bf(lora.at[0:64].set(w['a_w_decay'][0])[:, perm])
    out['wiclr_pad'] = bf(lora.at[64:128].set(w['a_w_iclr'][0])[:, perm])
    out['wgate_pad'] = bf(lora.at[128:256].set(w['a_w_gate'][0])[:, perm])
    out['mu_p'] = w['a_mu'][0][cols].reshape(1, -1)
    for name in ('a_w0', 'a_a0', 'a_k_a', 'a_r_k', 'a_lnx_g', 'a_lnx_b'):
        out[name + '_p'] = w[name][0].reshape(-1)[perm]
    for name in ('ffn_gate', 'ffn_up', 'ffn_down', 'w_in1', 'w_out1', 'c_w_a',
                 'c_w_x', 'moe_gate', 'moe_up', 'moe_down'):
        out[name] = bf(w[name][0])
    in_cols = np.concatenate([cols, np.arange(A_COLS, A_COLS + B_COLS)])
    out['w_in0'] = bf(w['w_in0'][0][:, in_cols])
    out_rows = np.concatenate([perm, np.arange(A_WIDTH, A_WIDTH + B_V_WIDTH)])
    out['w_out0'] = bf(w['w_out0'][0][out_rows])
    out['router_pad'] = jnp.zeros((w['moe_router'].shape[1], LANES), F32).at[:, :N_EXPERTS].set(
        w['moe_router'][0])
    return out


def _layer0(x, bn, length, pos0, shift, wkv, ret, w):
    t = bn * length
    cols = _rwkv_cols()
    pa, pb = _norm_matmul(x, w['norm_mix0'][0], w['w_in0'], (A_COLS, B_COLS))
    tm = _tile(t, 256) if length >= 256 else _tile(t, 256, length)
    shift = shift.astype(F32)[:, cols]
    if length >= tm:
        starts = jnp.arange(t // tm) * tm
        before = pa[jnp.maximum(starts - 1, 0)]
        first = jnp.where((starts % length == 0)[:, None], shift[starts // length], before)
        ext = jnp.zeros((t // tm, SUBLANES, A_COLS), F32).at[:, 0].set(first).reshape(-1, A_COLS)
    else:
        ext = jnp.repeat(shift, length, axis=0)
    row = lambda a: a.reshape(1, -1)
    cm = bn * A_HEADS <= LANES and length % LANES == 0 and tm % LANES == 0 and length >= tm
    outs = _rwkv_prep(
        pa, ext, length, tm,
        (w['mu_p'], row(w['a_w0_p']), w['wdec_pad'], row(w['a_a0_p']), w['wiclr_pad'],
         w['wgate_pad']), cm)
    r, dec, k, a, v, g = outs[:6]
    ops = outs[6:] if cm else (r, dec, k, a, v)
    y, wkv_new = _rwkv_recurrence(ops, w['a_k_k'][0], w['a_k_a'][0], wkv, bn, length, cm)
    yb, ret_new = _retention(pb, ret, bn, length, pos0)
    x = _mix_out(x, y, r, k, a, v, g, yb, w['a_k_a_p'], w['a_r_k_p'], w['a_lnx_g_p'],
                 w['a_lnx_b_p'], w['ones'], w['w_out0'], length, cm)
    x = _ffn(x, w['norm_ffn0'][0], w['ffn_gate'], w['ffn_up'], w['ffn_down'])
    shift_new = pa.reshape(bn, length, A_COLS)[:, -1][:, np.argsort(cols)]
    return x, shift_new, wkv_new, ret_new


def _layer1_mixer(x, bn, length, conv, hlru, w):
    assert length >= CONV_W - 1
    gate, xb = _norm_matmul(x, w['norm_mix1'][0], w['w_in1'], (D_RNN, D_RNN))
    tail0 = jnp.zeros((bn, SUBLANES, D_RNN), F32).at[:, SUBLANES - (CONV_W - 1):].set(conv.astype(F32))
    row = lambda a: a[0].reshape(1, -1)
    y, h_last = _rglru(gate, xb, tail0, hlru.astype(F32).reshape(bn, 1, D_RNN),
                       (w['c_conv_w'][0], row(w['c_conv_b']), w['c_w_a'], row(w['c_b_a']),
                        w['c_w_x'], row(w['c_b_x']), row(w['c_lambda'])), bn, length)
    x = _matmul_res(y, w['w_out1'], x)
    conv_new = xb.reshape(bn, length, D_RNN)[:, length - (CONV_W - 1):]
    return x, conv_new, h_last.reshape(bn, D_RNN)


def kernel(x_prompt, x_sample, state_rwkv_shift, state_rwkv_wkv, state_ret, state_lru_conv, state_lru_h, norm_mix0, w_in0, a_mu, a_w0, a_w_decay, a_a0, a_w_iclr, a_w_gate, a_k_k, a_k_a, a_r_k, a_lnx_g, a_lnx_b, w_out0, norm_ffn0, ffn_gate, ffn_up, ffn_down, norm_mix1, w_in1, c_conv_w, c_conv_b, c_w_a, c_b_a, c_w_x, c_b_x, c_lambda, w_out1, norm_ffn1, moe_router, moe_gate, moe_up, moe_down, norm_final):
    w = _prep_weights(dict(
        norm_mix0=norm_mix0, w_in0=w_in0, a_mu=a_mu, a_w0=a_w0, a_w_decay=a_w_decay, a_a0=a_a0,
        a_w_iclr=a_w_iclr, a_w_gate=a_w_gate, a_k_k=a_k_k, a_k_a=a_k_a, a_r_k=a_r_k, a_lnx_g=a_lnx_g,
        a_lnx_b=a_lnx_b, w_out0=w_out0, norm_ffn0=norm_ffn0, ffn_gate=ffn_gate, ffn_up=ffn_up,
        ffn_down=ffn_down, norm_mix1=norm_mix1, w_in1=w_in1, c_conv_w=c_conv_w, c_conv_b=c_conv_b,
        c_w_a=c_w_a, c_b_a=c_b_a, c_w_x=c_w_x, c_b_x=c_b_x, c_lambda=c_lambda, w_out1=w_out1,
        norm_ffn1=norm_ffn1, moe_router=moe_router, moe_gate=moe_gate, moe_up=moe_up,
        moe_down=moe_down))
    dt = x_prompt.dtype
    d = x_prompt.shape[-1]
    bp, lp = x_prompt.shape[:2]
    bs, ls = x_sample.shape[:2]
    zeros = lambda shape: jnp.zeros(shape, F32)
    groups = [
        (x_prompt.reshape(bp * lp, d), bp, lp, 0, zeros((bp, A_COLS)),
         zeros((bp, A_HEADS, A_HEAD_DIM, A_HEAD_DIM)), zeros((bp, B_HEADS, B_QK_DIM, B_V_DIM)),
         zeros((bp, CONV_W - 1, D_RNN)), zeros((bp, D_RNN))),
        (x_sample.reshape(bs * ls, d), bs, ls, PAST_LEN, state_rwkv_shift[0], state_rwkv_wkv[0],
         state_ret[0], state_lru_conv[0], state_lru_h[0]),
    ]
    xs, states, hs, routes = [], [], [], []
    for x, bn, length, pos0, shift, wkv, ret, conv, hlru in groups:
        x, s_shift, s_wkv, s_ret = _layer0(x, bn, length, pos0, shift, wkv, ret, w)
        x, s_conv, s_h = _layer1_mixer(x, bn, length, conv, hlru, w)
        h, route = _router(x, w['norm_ffn1'][0], w['router_pad'])
        xs.append(x)
        hs.append(h)
        routes.append(route)
        states.append((s_shift, s_wkv, s_ret, s_conv, s_h))
    h_all = jnp.concatenate(hs, 0)
    route_all = jnp.concatenate(routes, 0)
    tm_moe = 512 if h_all.shape[0] >= 4096 else 64
    tok_sorted, block_e, block_valid, dest = _route_plan(route_all, tm_moe)
    y_sorted = _moe(block_e, block_valid, tok_sorted, h_all, w['moe_gate'], w['moe_up'],
                    w['moe_down'], tm_moe)
    outs = []
    off = 0
    for x, route in zip(xs, routes):
        t = x.shape[0]
        tm = _tile(t, 256)
        pos = _tile_pos(dest[off:off + t], tm)
        outs.append(_combine(pos, y_sorted, x, route, norm_final, tm))
        off += t
    y_prompt = outs[0].reshape(bp, lp, d)
    y_sample = outs[1].reshape(bs, ls, d)
    st_p = tuple(s[None].astype(dt) for s in states[0])
    st_s = tuple(s[None].astype(dt) for s in states[1])
    return (y_prompt, y_sample) + st_p + st_s
```

```python
import functools
import math

import jax
import jax.numpy as jnp
import numpy as np
from jax import lax
from jax.experimental import pallas as pl
from jax.experimental.pallas import tpu as pltpu

F32 = jnp.float32
BF16 = jnp.bfloat16

A_HEADS = 8
A_HEAD_DIM = 64
A_WIDTH = A_HEADS * A_HEAD_DIM
A_LORA_COLS = 256
A_COLS = 3 * A_WIDTH + A_LORA_COLS
A_GN_EPS = A_HEAD_DIM * 1e-5
B_HEADS = 4
B_QK_DIM = 64
B_V_DIM = 128
B_QK_WIDTH = B_HEADS * B_QK_DIM
B_V_WIDTH = B_HEADS * B_V_DIM
B_COLS = 2 * B_QK_WIDTH + 2 * B_V_WIDTH
RET_CHUNK = 64
ROPE_BASE = 10000.0
D_RNN = 1280
C_BLOCKS = 10
C_BLOCK_DIM = D_RNN // C_BLOCKS
CONV_W = 4
LRU_C = 8.0
N_EXPERTS = 8
TOP_K = 2
NORM_EPS = 1e-6
PAST_LEN = 16384

LANES = 128
SUBLANES = 8
VMEM_PHYSICAL_BYTES = 64 * 1024 * 1024
VMEM_BUDGET_BYTES = VMEM_PHYSICAL_BYTES - 4 * 1024 * 1024


def _tile(n, pref, mult=SUBLANES):
    t = min(pref, n)
    while t > mult and (n % t or t % mult):
        t -= 1
    assert n % t == 0 and t % mult == 0, (n, pref, mult)
    return t


def _params(sem, est_bytes):
    limit = int(min(max(est_bytes * 5 // 4 + (4 << 20), 32 << 20), VMEM_BUDGET_BYTES))
    return pltpu.CompilerParams(dimension_semantics=sem, vmem_limit_bytes=limit)


def _rms(x, g):
    return x * lax.rsqrt(jnp.mean(x * x, -1, keepdims=True) + NORM_EPS) * g


def _dot(a, b):
    return jnp.dot(a, b, preferred_element_type=F32)


def _sigmoid(x):
    return 0.5 * (jnp.tanh(0.5 * x) + 1.0)


def _seg_sum(x, ones):
    hi = x.astype(BF16)
    lo = (x - hi.astype(F32)).astype(BF16)
    return _dot(hi, ones) + _dot(lo, ones)


def _norm_matmul_kernel(x_ref, g_ref, w_ref, *o_refs, splits):
    h = _rms(x_ref[...], g_ref[...]).astype(BF16)
    off = 0
    for o_ref, n in zip(o_refs, splits):
        o_ref[...] = _dot(h, w_ref[:, off:off + n])
        off += n


def _norm_matmul(x, g, w, splits):
    t, d = x.shape
    n = w.shape[1]
    tm = _tile(t, 512)
    est = 2 * (tm * d * 4 + d * n * 2 + tm * n * 4) + tm * n * 4
    return pl.pallas_call(
        functools.partial(_norm_matmul_kernel, splits=splits),
        grid=(t // tm,),
        in_specs=[pl.BlockSpec((tm, d), lambda i: (i, 0)),
                  pl.BlockSpec((1, d), lambda i: (0, 0)),
                  pl.BlockSpec((d, n), lambda i: (0, 0))],
        out_specs=[pl.BlockSpec((tm, s), lambda i: (i, 0)) for s in splits],
        out_shape=[jax.ShapeDtypeStruct((t, s), F32) for s in splits],
        compiler_params=_params(("arbitrary",), est),
        name="norm_matmul",
    )(x, g.reshape(1, d), w)


def _rwkv_prep_kernel(p_ref, ext_ref, mu_ref, w0_ref, wdec_ref, a0_ref, wiclr_ref, wgate_ref,
                      r_o, w_o, k_o, a_o, v_o, g_o, *cm_outs, seq_len, tm):
    p = p_ref[...]
    rolled = pltpu.roll(p, 1, 0)
    row = lax.broadcasted_iota(jnp.int32, (tm, 1), 0)
    if seq_len >= tm:
        prev = jnp.where(row == 0, ext_ref[0:1, :], rolled)
    else:
        prev = jnp.where(row % seq_len == 0, ext_ref[...], rolled)
    pm = p + (prev - p) * mu_ref[...]
    r = pm[:, 0:A_WIDTH]
    k = pm[:, A_WIDTH:2 * A_WIDTH]
    v = pm[:, 2 * A_WIDTH:3 * A_WIDTH]
    tail = pm[:, 3 * A_WIDTH:A_COLS]
    w_pre = w0_ref[...] + _dot(jnp.tanh(tail).astype(BF16), wdec_ref[...])
    w_log = jnp.minimum(w_pre, 0.0) - jnp.log1p(jnp.exp(-jnp.abs(w_pre))) - 0.5
    decay = jnp.exp(-jnp.exp(w_log))
    a = _sigmoid(a0_ref[...] + _dot(tail.astype(BF16), wiclr_ref[...]))
    g = _dot(_sigmoid(tail).astype(BF16), wgate_ref[...])
    r_o[...] = r
    w_o[...] = decay
    k_o[...] = k
    a_o[...] = a
    v_o[...] = v
    g_o[...] = g
    for o_ref, val in zip(cm_outs, (r, decay, k, a, v)):
        o_ref[...] = val.T


def _rwkv_prep(pa, ext, seq_len, tm, wts, channel_major):
    t = pa.shape[0]
    full = lambda shape: pl.BlockSpec(shape, lambda i: (0,) * len(shape))
    ext_rows = SUBLANES if seq_len >= tm else tm
    ncm = 5 if channel_major else 0
    est = 2 * (tm * A_COLS * 4 * 2 + (6 + ncm) * tm * A_WIDTH * 4) + 12 * tm * A_WIDTH * 4
    out_specs = [pl.BlockSpec((tm, A_WIDTH), lambda i: (i, 0))] * 6
    out_shape = [jax.ShapeDtypeStruct((t, A_WIDTH), F32)] * 6
    if channel_major:
        nt = seq_len // tm
        out_specs += [pl.BlockSpec((A_WIDTH, tm), lambda i: (i // nt, i % nt))] * ncm
        out_shape += [jax.ShapeDtypeStruct((t // seq_len * A_WIDTH, seq_len), F32)] * ncm
    return pl.pallas_call(
        functools.partial(_rwkv_prep_kernel, seq_len=seq_len, tm=tm),
        grid=(t // tm,),
        in_specs=[pl.BlockSpec((tm, A_COLS), lambda i: (i, 0)),
                  pl.BlockSpec((ext_rows, A_COLS), lambda i: (i, 0)),
                  full((1, A_COLS)), full((1, A_WIDTH)), full((A_LORA_COLS, A_WIDTH)),
                  full((1, A_WIDTH)), full((A_LORA_COLS, A_WIDTH)), full((A_LORA_COLS, A_WIDTH))],
        out_specs=out_specs,
        out_shape=out_shape,
        compiler_params=_params(("arbitrary",), est),
        name="rwkv_prep",
    )(pa, ext, *wts)


def _rwkv_scan_kernel(r_in, w_in, k_in, a_in, v_in, kkp_ref, kap_ref, s0_ref, y_out, s_scr,
                      r_ref, w_ref, k_ref, kk_ref, b_ref, *cm_scr, tl, nv, dup, cm):
    tb = pl.program_id(1)
    nvg = nv // SUBLANES
    kdim = A_HEAD_DIM
    n = LANES // dup

    @pl.when(tb == 0)
    def _():
        s_scr[...] = s0_ref[...]

    if cm:
        v_scr, y_scr = cm_scr

        nseq = n // A_HEADS
        seq_rows = kdim * A_HEADS

        def tload(ref, chans):
            def first_row(c):
                off = c * A_HEADS
                return off if isinstance(c, int) else pl.multiple_of(off, A_HEADS)

            tiles = [ref[pl.ds(b * seq_rows + first_row(c), A_HEADS), :]
                     for c in chans for b in range(nseq)]
            return jnp.concatenate(tiles, axis=0).T

        def fill(c, ss):
            rows = pl.ds(pl.multiple_of(c * tl, tl), tl)
            r_ref[rows, :] = tload(r_in, [c] * dup)
            w_ref[rows, :] = tload(w_in, [c] * dup)
            kc = tload(k_in, [c] * dup)
            ac = tload(a_in, [c] * dup)
            kkc = kc * kkp_ref[pl.ds(c, 1), :]
            k_ref[rows, :] = kc * (1.0 + (ac - 1.0) * kap_ref[pl.ds(c, 1), :])
            kk_ref[rows, :] = kkc
            b_ref[rows, :] = ac
            return ss + kkc * kkc

        ss = lax.fori_loop(0, kdim, fill, jnp.zeros((tl, LANES), F32))
        denom = jnp.maximum(jnp.sqrt(ss), 1e-12)

        def normalise(c, carry):
            rows = pl.ds(pl.multiple_of(c * tl, tl), tl)
            kkn = kk_ref[rows, :] / denom
            kk_ref[rows, :] = kkn
            b_ref[rows, :] = -(kkn * b_ref[rows, :])
            return carry

        lax.fori_loop(0, kdim, normalise, 0)
        for r in range(nv):
            v_scr[pl.ds(r, tl, stride=nv), :] = tload(v_in, [vh * nv + r for vh in range(dup)])

        def vtile(t, vg):
            return v_scr[pl.ds(pl.multiple_of(t * nv, SUBLANES) + vg * SUBLANES, SUBLANES), :]

        def ystore(t, vg, val):
            y_scr[pl.ds(pl.multiple_of(t * nv, SUBLANES) + vg * SUBLANES, SUBLANES), :] = val
    else:
        def lanes(x):
            return jnp.concatenate([x] * dup, axis=-1) if dup > 1 else x

        def flat(x):
            return x.reshape(tl * kdim, LANES)

        k = lanes(k_in[...])
        a = lanes(a_in[...])
        kk = k * kkp_ref[...]
        kkn = kk / jnp.maximum(jnp.sqrt(jnp.sum(kk * kk, axis=1, keepdims=True)), 1e-12)
        r_ref[...] = flat(lanes(r_in[...]))
        w_ref[...] = flat(lanes(w_in[...]))
        k_ref[...] = flat(k * (1.0 + (a - 1.0) * kap_ref[...]))
        kk_ref[...] = flat(kkn)
        b_ref[...] = flat(-(kkn * a))

        def vtile(t, vg):
            return v_in[t, vg * SUBLANES:(vg + 1) * SUBLANES, :]

        def ystore(t, vg, val):
            y_out[t, vg * SUBLANES:(vg + 1) * SUBLANES, :] = val

    def bcast(ref, t, k):
        row = k * tl + t if cm else t * kdim + k
        return jnp.broadcast_to(ref[pl.ds(row, 1), :], (SUBLANES, LANES))

    zeros = lambda: [jnp.zeros((SUBLANES, LANES), F32) for _ in range(nvg)]

    sa0 = zeros()
    for k in range(kdim):
        kk_row = bcast(kk_ref, 0, k)
        for vg in range(nvg):
            sa0[vg] = sa0[vg] + s_scr[k, vg * SUBLANES:(vg + 1) * SUBLANES, :] * kk_row

    def step(t, sa):
        t_next = jnp.minimum(t + 1, tl - 1)
        vv = [vtile(t, vg) for vg in range(nvg)]
        y = zeros()
        sa_next = zeros()
        for k in range(kdim):
            w_row = bcast(w_ref, t, k)
            b_row = bcast(b_ref, t, k)
            k_row = bcast(k_ref, t, k)
            r_row = bcast(r_ref, t, k)
            kk_row = bcast(kk_ref, t_next, k)
            for vg in range(nvg):
                rows = slice(vg * SUBLANES, (vg + 1) * SUBLANES)
                s_new = s_scr[k, rows, :] * w_row + sa[vg] * b_row + vv[vg] * k_row
                s_scr[k, rows, :] = s_new
                y[vg] = y[vg] + s_new * r_row
                sa_next[vg] = sa_next[vg] + s_new * kk_row
        for vg in range(nvg):
            ystore(t, vg, y[vg])
        return tuple(sa_next)

    lax.fori_loop(0, tl, step, tuple(sa0))

    if cm:
        for r in range(nv):
            yt = y_scr[pl.ds(r, tl, stride=nv), :].T
            for vh in range(dup):
                for b in range(nseq):
                    y_out[pl.ds(b * seq_rows + (vh * nv + r) * A_HEADS, A_HEADS), :] = (
                        yt[vh * n + b * A_HEADS:vh * n + (b + 1) * A_HEADS, :])


def _rwkv_scan(ops, kkp, kap, s0, length, tl, nv, dup, cm):
    kdim = A_HEAD_DIM
    nl = s0.shape[-1]
    nbk = LANES // dup
    pspec = pl.BlockSpec((kdim, LANES), lambda g, tb: (0, g))
    sspec = pl.BlockSpec((kdim, nv, LANES), lambda g, tb: (0, 0, g))
    scratch = [pltpu.VMEM((tl * kdim, LANES), F32)] * 5
    if cm:
        assert nl == LANES and tl == LANES
        inspec = pl.BlockSpec((nbk * kdim, tl), lambda g, tb: (0, tb), pipeline_mode=pl.Buffered(1))
        in_specs = [inspec] * 5
        yspec = pl.BlockSpec((nbk * kdim, tl), lambda g, tb: (0, tb))
        yshape = jax.ShapeDtypeStruct((nbk * kdim, length), F32)
        scratch += [pltpu.VMEM((tl * nv, LANES), F32)] * 2
        est = 5 * nbk * kdim * tl * 4 + 2 * nbk * kdim * tl * 4 + 2 * tl * nv * LANES * 4
    else:
        kspec = pl.BlockSpec((tl, kdim, nbk), lambda g, tb: (tb, 0, g))
        yspec = pl.BlockSpec((tl, nv, LANES), lambda g, tb: (tb, 0, g))
        in_specs = [kspec] * 4 + [yspec]
        yshape = jax.ShapeDtypeStruct((length, nv, nl), F32)
        est = 2 * (4 * tl * kdim * LANES * 4 + 2 * tl * nv * LANES * 4) + 4 * tl * kdim * LANES * 4
    est += 5 * kdim * nv * LANES * 4 + 5 * tl * kdim * LANES * 4
    return pl.pallas_call(
        functools.partial(_rwkv_scan_kernel, tl=tl, nv=nv, dup=dup, cm=cm),
        grid=(nl // LANES, length // tl),
        in_specs=in_specs + [pspec, pspec,
                             pl.BlockSpec((kdim, nv, LANES), lambda g, tb: (0, 0, g),
                                          pipeline_mode=pl.Buffered(1))],
        out_specs=[yspec, sspec],
        out_shape=[yshape, jax.ShapeDtypeStruct((kdim, nv, nl), F32)],
        scratch_shapes=scratch,
        compiler_params=_params(("arbitrary", "arbitrary"), est),
        name="rwkv_scan",
    )(*ops, kkp, kap, s0)


def _rwkv_recurrence(ops, k_k, k_a, wkv0, bn, length, cm):
    n = bn * A_HEADS
    dup = max(1, LANES // n)
    nv = A_HEAD_DIM // dup
    assert nv % SUBLANES == 0 and (n * dup) % LANES == 0, (bn, n)

    def to_t(x):
        x = x.reshape(bn, length * A_WIDTH).T
        return x.reshape(length, A_HEAD_DIM, n)

    def from_t(y):
        return y.reshape(length * A_WIDTH, bn).T.reshape(bn * length, A_WIDTH)

    def param_t(p):
        p = p.reshape(A_HEADS, A_HEAD_DIM).T.astype(F32)
        p = jnp.tile(p, (1, bn)) if cm else jnp.repeat(p, bn, axis=1)
        return jnp.tile(p, (1, dup))

    def vpack(x):
        lead = x.shape[0]
        x = x.reshape(lead, dup, nv, n)
        return jnp.transpose(x, (0, 2, 1, 3)).reshape(lead, nv, dup * n)

    def vunpack(x):
        lead = x.shape[0]
        x = x.reshape(lead, nv, dup, n)
        return jnp.transpose(x, (0, 2, 1, 3)).reshape(lead, A_HEAD_DIM, n)

    s_perm = (3, 2, 0, 1) if cm else (3, 2, 1, 0)
    s0 = jnp.transpose(wkv0.astype(F32), s_perm).reshape(A_HEAD_DIM, A_HEAD_DIM, n)
    s0 = vpack(s0)
    if cm:
        y, s_last = _rwkv_scan(ops, param_t(k_k), param_t(k_a), s0, length, LANES, nv, dup, True)
    else:
        r, w, k, a, v = ops
        ops_t = [to_t(x) for x in (r, w, k, a)] + [vpack(to_t(v))]
        tl = _tile(length, 32, 1)
        y_t, s_last = _rwkv_scan(ops_t, param_t(k_k), param_t(k_a), s0, length, tl, nv, dup, False)
        y = from_t(vunpack(y_t))
    if cm:
        s_last = vunpack(s_last).reshape(A_HEAD_DIM, A_HEAD_DIM, bn, A_HEADS)
        s_last = jnp.transpose(s_last, (2, 3, 1, 0))
    else:
        s_last = vunpack(s_last).reshape(A_HEAD_DIM, A_HEAD_DIM, A_HEADS, bn)
        s_last = jnp.transpose(s_last, (3, 2, 1, 0))
    return y, s_last


def _retention_kernel(p_ref, cos_ref, sin_ref, qdec_ref, kdec_ref, intra_ref, cdec_ref, s0_ref,
                      y_ref, sout_ref, s_scr, *, nsub, c, chained):
    ci = pl.program_id(1)
    rows = nsub * c

    if chained:
        @pl.when(ci == 0)
        def _():
            s_scr[...] = s0_ref[...]

    lane = lax.broadcasted_iota(jnp.int32, (rows, 2 * B_QK_WIDTH), 1)
    half = B_QK_DIM // 2
    qk = p_ref[:, 0:2 * B_QK_WIDTH]
    swapped = jnp.where((lane % B_QK_DIM) < half, pltpu.roll(qk, 2 * B_QK_WIDTH - half, 1),
                        pltpu.roll(qk, half, 1))
    rot = qk * cos_ref[...] + swapped * sin_ref[...]
    q = rot[:, :B_QK_WIDTH] * (B_QK_DIM ** -0.5)
    k = rot[:, B_QK_WIDTH:]
    qd = (q * qdec_ref[...]).astype(BF16)
    kd = (k * kdec_ref[...]).astype(BF16)
    qb = q.astype(BF16)
    kb = k.astype(BF16)
    for h in range(B_HEADS):
        qs = slice(h * B_QK_DIM, (h + 1) * B_QK_DIM)
        vs = slice(2 * B_QK_WIDTH + h * B_V_DIM, 2 * B_QK_WIDTH + (h + 1) * B_V_DIM)
        gs = slice(2 * B_QK_WIDTH + B_V_WIDTH + h * B_V_DIM,
                   2 * B_QK_WIDTH + B_V_WIDTH + (h + 1) * B_V_DIM)
        if chained:
            st = s_scr[0, h]
        for u in range(nsub):
            rs = slice(u * c, (u + 1) * c)
            if not chained:
                st = s0_ref[u, h]
            vh = p_ref[rs, vs].astype(BF16)
            gh = p_ref[rs, gs]
            scores = lax.dot_general(qb[rs, qs], kb[rs, qs], (((1,), (1,)), ((), ())),
                                     preferred_element_type=F32) * intra_ref[h]
            o = _dot(scores.astype(BF16), vh) + _dot(qd[rs, qs], st.astype(BF16))
            st = st * cdec_ref[h] + lax.dot_general(
                kd[rs, qs], vh, (((0,), (0,)), ((), ())), preferred_element_type=F32)
            o = o * lax.rsqrt(jnp.mean(o * o, -1, keepdims=True) + NORM_EPS)
            y_ref[rs, h * B_V_DIM:(h + 1) * B_V_DIM] = o * (gh * _sigmoid(gh))
            if not chained:
                sout_ref[u, h] = st
        if chained:
            s_scr[0, h] = st

    if chained:
        @pl.when(ci == pl.num_programs(1) - 1)
        def _():
            sout_ref[...] = s_scr[...]


def _retention(pb, s0, bn, length, pos0):
    c = math.gcd(length, RET_CHUNK)
    nc = length // c
    chained = nc > 1
    nsub = _tile(nc, 8, 1) if chained else _tile(bn, 8, 1)
    sb = 1 if chained else nsub
    nstep = nc // nsub if chained else 1
    half = B_QK_DIM // 2
    inv = ROPE_BASE ** (-jnp.arange(half, dtype=F32) / half)
    pos = (pos0 + jnp.arange(length)).astype(F32)
    ang = pos[:, None] * inv[None, :]
    cos, sin = jnp.cos(ang), jnp.sin(ang)
    cosf = jnp.tile(jnp.concatenate([cos, cos], -1), (1, 2 * B_HEADS))
    sinf = jnp.tile(jnp.concatenate([-sin, sin], -1), (1, 2 * B_HEADS))
    log_g = jnp.log1p(-jnp.exp2(-5.0 - jnp.arange(B_HEADS, dtype=F32)))
    idx = jnp.arange(c, dtype=F32)
    diff = idx[:, None] - idx[None, :]
    intra = jnp.where(diff >= 0, jnp.exp(jnp.maximum(diff, 0.0) * log_g[:, None, None]), 0.0)
    q_dec = jnp.exp((idx + 1.0)[:, None] * log_g[None, :])
    k_dec = jnp.exp((c - 1.0 - idx)[:, None] * log_g[None, :])
    c_dec = jnp.exp(c * log_g)
    rows = nsub * c
    qdec = jnp.tile(jnp.repeat(q_dec, B_QK_DIM, axis=1), (nsub, 1))
    kdec = jnp.tile(jnp.repeat(k_dec, B_QK_DIM, axis=1), (nsub, 1))
    cdec = jnp.broadcast_to(c_dec[:, None, None], (B_HEADS, 1, B_V_DIM))
    if not chained:
        cosf = jnp.tile(cosf, (nsub, 1))
        sinf = jnp.tile(sinf, (nsub, 1))

    full = lambda shape: pl.BlockSpec(shape, lambda i, j: (0,) * len(shape))
    tspec = pl.BlockSpec((rows, 2 * B_QK_WIDTH), lambda i, j: (j, 0))
    sspec = pl.BlockSpec((sb, B_HEADS, B_QK_DIM, B_V_DIM), lambda i, j: (i, 0, 0, 0))
    est = 2 * (rows * B_COLS * 4 + rows * B_V_WIDTH * 4 + 2 * sb * B_HEADS * B_QK_DIM * B_V_DIM * 4
               + 2 * rows * 2 * B_QK_WIDTH * 4) \
        + 3 * sb * B_HEADS * B_QK_DIM * B_V_DIM * 4 + 8 * rows * B_COLS * 4
    return pl.pallas_call(
        functools.partial(_retention_kernel, nsub=nsub, c=c, chained=chained),
        grid=(bn // sb, nstep),
        in_specs=[pl.BlockSpec((rows, B_COLS), lambda i, j: (i * nstep + j, 0)),
                  tspec, tspec,
                  full((rows, B_QK_WIDTH)), full((rows, B_QK_WIDTH)), full((B_HEADS, c, c)),
                  full((B_HEADS, 1, B_V_DIM)), sspec],
        out_specs=[pl.BlockSpec((rows, B_V_WIDTH), lambda i, j: (i * nstep + j, 0)), sspec],
        out_shape=[jax.ShapeDtypeStruct((bn * length, B_V_WIDTH), F32),
                   jax.ShapeDtypeStruct((bn, B_HEADS, B_QK_DIM, B_V_DIM), F32)],
        scratch_shapes=[pltpu.VMEM((sb, B_HEADS, B_QK_DIM, B_V_DIM), F32)],
        compiler_params=_params(("arbitrary", "arbitrary"), est),
        name="retention",
    )(pb, cosf, sinf, qdec, kdec, intra, cdec, s0.astype(F32))


def _mix_out_kernel(x_ref, y_ref, r_ref, k_ref, a_ref, v_ref, g_ref, yb_ref, ka_ref, rk_ref, lg_ref,
                    lb_ref, ones_ref, w_ref, o_ref, *, y_cm):
    ones = ones_ref[...]
    y = y_ref[...].T if y_cm else y_ref[...]
    inv_d = 1.0 / A_HEAD_DIM
    mean = _seg_sum(y, ones) * inv_d
    d = y - mean
    var = _seg_sum(d * d, ones) * inv_d
    yn = d * lax.rsqrt(var + A_GN_EPS) * lg_ref[...] + lb_ref[...]
    v = v_ref[...]
    kmod = k_ref[...] * (1.0 + (a_ref[...] - 1.0) * ka_ref[...])
    bonus = _seg_sum(r_ref[...] * kmod * rk_ref[...], ones) * v
    ya = ((yn + bonus) * g_ref[...]).astype(BF16)
    o_ref[...] = (x_ref[...] + _dot(ya, w_ref[0:A_WIDTH, :])
                  + _dot(yb_ref[...].astype(BF16), w_ref[A_WIDTH:, :]))


def _mix_out(x, y, r, k, a, v, g, yb, k_a, r_k, lnx_g, lnx_b, ones, w_out, seq_len, y_cm):
    t, d = x.shape
    tm = _tile(seq_len, 512, LANES) if y_cm else _tile(t, 512)
    row = lambda n: pl.BlockSpec((tm, n), lambda i: (i, 0))
    full = lambda shape: pl.BlockSpec(shape, lambda i: (0,) * len(shape))
    est = 2 * (2 * tm * d * 4 + 7 * tm * A_WIDTH * 4 + (A_WIDTH + B_V_WIDTH) * d * 2) + 10 * tm * A_WIDTH * 4
    vec = lambda p: p.reshape(1, A_WIDTH)
    nt = seq_len // tm if y_cm else 1
    yspec = pl.BlockSpec((A_WIDTH, tm), lambda i: (i // nt, i % nt)) if y_cm else row(A_WIDTH)
    return pl.pallas_call(
        functools.partial(_mix_out_kernel, y_cm=y_cm),
        grid=(t // tm,),
        in_specs=[row(d), yspec] + [row(A_WIDTH)] * 6 + [full((1, A_WIDTH))] * 4
                 + [full((A_WIDTH, A_WIDTH)), full((A_WIDTH + B_V_WIDTH, d))],
        out_specs=row(d),
        out_shape=jax.ShapeDtypeStruct((t, d), F32),
        compiler_params=_params(("arbitrary",), est),
        name="mix_out",
    )(x, y, r, k, a, v, g, yb, vec(k_a), vec(r_k), vec(lnx_g), vec(lnx_b), ones, w_out)


def _ffn_kernel(x_ref, g_ref, wg_ref, wu_ref, wd_ref, o_ref, h_scr):
    j = pl.program_id(1)

    @pl.when(j == 0)
    def _():
        x = x_ref[...]
        h_scr[...] = _rms(x, g_ref[...]).astype(BF16)
        o_ref[...] = x

    h = h_scr[...]
    a = _dot(h, wg_ref[...])
    b = _dot(h, wu_ref[...])
    m = (a * _sigmoid(a) * b).astype(BF16)
    o_ref[...] += _dot(m, wd_ref[...])


def _ffn(x, g, wg, wu, wd):
    t, d = x.shape
    ff = wg.shape[1]
    tm = _tile(t, 512)
    tf = _tile(ff, 1536, LANES)
    est = 2 * (2 * tm * d * 4 + 3 * d * tf * 2) + tm * d * 2 + 3 * tm * tf * 4
    return pl.pallas_call(
        _ffn_kernel,
        grid=(t // tm, ff // tf),
        in_specs=[pl.BlockSpec((tm, d), lambda i, j: (i, 0)),
                  pl.BlockSpec((1, d), lambda i, j: (0, 0)),
                  pl.BlockSpec((d, tf), lambda i, j: (0, j)),
                  pl.BlockSpec((d, tf), lambda i, j: (0, j)),
                  pl.BlockSpec((tf, d), lambda i, j: (j, 0))],
        out_specs=pl.BlockSpec((tm, d), lambda i, j: (i, 0)),
        out_shape=jax.ShapeDtypeStruct((t, d), F32),
        scratch_shapes=[pltpu.VMEM((tm, d), BF16)],
        compiler_params=_params(("arbitrary", "arbitrary"), est),
        name="ffn",
    )(x, g.reshape(1, d), wg, wu, wd)


def _rglru_kernel(gate_ref, xb_ref, tail0_ref, h0_ref, cw_ref, cb_ref, wa_ref, ba_ref, wx_ref,
                  bx_ref, lam_ref, y_ref, hlast_ref, tail_scr, h_scr, a_scr, b_scr, hs_scr, *, tl):
    tb = pl.program_id(1)

    @pl.when(tb == 0)
    def _():
        tail_scr[...] = tail0_ref[0]
        h_scr[...] = h0_ref[0]

    xb = xb_ref[...]
    full = jnp.concatenate([tail_scr[...], xb], axis=0)
    xc = cb_ref[...]
    for j in range(CONV_W):
        shift = CONV_W - 1 - j
        term = pltpu.roll(full, shift, 0) if shift else full
        xc = xc + term[SUBLANES:, :] * cw_ref[j:j + 1, :]
    tail_scr[...] = xb[tl - SUBLANES:, :]
    xcb = xc.astype(BF16)
    for n in range(C_BLOCKS):
        sl = slice(n * C_BLOCK_DIM, (n + 1) * C_BLOCK_DIM)
        xn = xcb[:, sl]
        r = _sigmoid(_dot(xn, wa_ref[n]) + ba_ref[:, sl])
        i = _sigmoid(_dot(xn, wx_ref[n]) + bx_ref[:, sl])
        lam = lam_ref[:, sl]
        softplus_neg_lam = jnp.maximum(-lam, 0.0) + jnp.log1p(jnp.exp(-jnp.abs(lam)))
        log_a = -LRU_C * r * softplus_neg_lam
        a = jnp.exp(log_a)
        gain = jnp.sqrt(-jnp.tanh(log_a) * (a * a + 1.0))
        a_scr[:, sl] = a
        b_scr[:, sl] = gain * i * xc[:, sl]

    def row(t, h):
        h = a_scr[pl.ds(t, 1), :] * h + b_scr[pl.ds(t, 1), :]
        hs_scr[pl.ds(t, 1), :] = h
        return h

    h = lax.fori_loop(0, tl, row, h_scr[...], unroll=8)
    h_scr[...] = h
    hlast_ref[0] = h
    gate = gate_ref[...]
    cdf = 0.5 * (1.0 + jnp.tanh(math.sqrt(2.0 / math.pi) * (gate + 0.044715 * (gate * gate * gate))))
    y_ref[...] = (gate * cdf * hs_scr[...]).astype(BF16)


def _rglru(gate, xb, tail0, h0, wts, bn, length):
    tl = _tile(length, 256)
    nt = length // tl
    full = lambda shape: pl.BlockSpec(shape, lambda i, j: (0,) * len(shape))
    row = pl.BlockSpec((tl, D_RNN), lambda i, j: (i * nt + j, 0))
    est = 2 * (3 * tl * D_RNN * 4 + 2 * C_BLOCKS * C_BLOCK_DIM * C_BLOCK_DIM * 2) + 10 * tl * D_RNN * 4
    return pl.pallas_call(
        functools.partial(_rglru_kernel, tl=tl),
        grid=(bn, nt),
        in_specs=[row, row,
                  pl.BlockSpec((1, SUBLANES, D_RNN), lambda i, j: (i, 0, 0)),
                  pl.BlockSpec((1, 1, D_RNN), lambda i, j: (i, 0, 0)),
                  full((CONV_W, D_RNN)), full((1, D_RNN)),
                  full((C_BLOCKS, C_BLOCK_DIM, C_BLOCK_DIM)), full((1, D_RNN)),
                  full((C_BLOCKS, C_BLOCK_DIM, C_BLOCK_DIM)), full((1, D_RNN)),
                  full((1, D_RNN))],
        out_specs=[row, pl.BlockSpec((1, 1, D_RNN), lambda i, j: (i, 0, 0))],
        out_shape=[jax.ShapeDtypeStruct((bn * length, D_RNN), BF16),
                   jax.ShapeDtypeStruct((bn, 1, D_RNN), F32)],
        scratch_shapes=[pltpu.VMEM((SUBLANES, D_RNN), F32), pltpu.VMEM((1, D_RNN), F32),
                        pltpu.VMEM((tl, D_RNN), F32), pltpu.VMEM((tl, D_RNN), F32),
                        pltpu.VMEM((tl, D_RNN), F32)],
        compiler_params=_params(("arbitrary", "arbitrary"), est),
        name="rglru",
    )(gate, xb, tail0, h0, *wts)


def _matmul_res_kernel(y_ref, w_ref, x_ref, o_ref):
    o_ref[...] = x_ref[...] + _dot(y_ref[...], w_ref[...])


def _matmul_res(y, w, x):
    t, d = x.shape
    kdim = y.shape[1]
    tm = _tile(t, 512, 16)
    est = 2 * (tm * kdim * 2 + kdim * d * 2 + 2 * tm * d * 4)
    return pl.pallas_call(
        _matmul_res_kernel,
        grid=(t // tm,),
        in_specs=[pl.BlockSpec((tm, kdim), lambda i: (i, 0)),
                  pl.BlockSpec((kdim, d), lambda i: (0, 0)),
                  pl.BlockSpec((tm, d), lambda i: (i, 0))],
        out_specs=pl.BlockSpec((tm, d), lambda i: (i, 0)),
        out_shape=jax.ShapeDtypeStruct((t, d), F32),
        compiler_params=_params(("arbitrary",), est),
        name="matmul_res",
    )(y, w, x)


def _router_kernel(x_ref, g_ref, wr_ref, h_ref, route_ref, *, tm):
    h = _rms(x_ref[...], g_ref[...])
    h_ref[...] = h
    logits = jnp.dot(h, wr_ref[...], preferred_element_type=F32, precision=lax.Precision.HIGHEST)
    lane = lax.broadcasted_iota(jnp.int32, (tm, LANES), 1).astype(F32)
    neg = jnp.float32(-jnp.inf)
    lg = jnp.where(lane < N_EXPERTS, logits, neg)
    m1 = jnp.max(lg, -1, keepdims=True)
    i1 = jnp.min(jnp.where(lg == m1, lane, float(LANES)), -1, keepdims=True)
    lg2 = jnp.where(lane == i1, neg, lg)
    m2 = jnp.max(lg2, -1, keepdims=True)
    i2 = jnp.min(jnp.where(lg2 == m2, lane, float(LANES)), -1, keepdims=True)
    e = jnp.exp(m2 - m1)
    g1 = 1.0 / (1.0 + e)
    g2 = e / (1.0 + e)
    route_ref[...] = jnp.where(lane == 0, i1, jnp.where(lane == 1, i2, jnp.where(
        lane == 2, g1, jnp.where(lane == 3, g2, 0.0))))


def _router(x, g, wr_pad):
    t, d = x.shape
    tm = _tile(t, 512)
    est = 2 * (2 * tm * d * 4 + d * LANES * 4 + tm * LANES * 4) + 2 * tm * d * 4
    return pl.pallas_call(
        functools.partial(_router_kernel, tm=tm),
        grid=(t // tm,),
        in_specs=[pl.BlockSpec((tm, d), lambda i: (i, 0)),
                  pl.BlockSpec((1, d), lambda i: (0, 0)),
                  pl.BlockSpec((d, LANES), lambda i: (0, 0))],
        out_specs=[pl.BlockSpec((tm, d), lambda i: (i, 0)),
                   pl.BlockSpec((tm, LANES), lambda i: (i, 0))],
        out_shape=[jax.ShapeDtypeStruct((t, d), F32), jax.ShapeDtypeStruct((t, LANES), F32)],
        compiler_params=_params(("arbitrary",), est),
        name="router",
    )(x, g.reshape(1, d), wr_pad)


def _moe_kernel(be_ref, bv_ref, tok_hbm, h_hbm, wg_ref, wu_ref, wd_ref, o_ref,
                idx_smem, xbuf, xbf, sem_idx, sem_rows, *, tm):
    i = pl.program_id(0)
    j = pl.program_id(1)
    nb = pl.num_programs(0)
    valid = bv_ref[i] != 0
    slot = i % 2
    nxt = jnp.minimum(i + 1, nb - 1)

    def idx_copy(blk, s):
        return pltpu.make_async_copy(tok_hbm.at[pl.ds(blk * tm, tm)],
                                     idx_smem.at[pl.ds(s * tm, tm)], sem_idx.at[s])

    def row_copy(tok, s, g, u):
        return pltpu.make_async_copy(h_hbm.at[pl.ds(tok, 1)], xbuf.at[s, g, pl.ds(u, 1)],
                                     sem_rows.at[s])

    def issue_rows(s):
        def issue(g, c):
            for u in range(SUBLANES):
                row_copy(idx_smem[s * tm + g * SUBLANES + u], s, g, u).start(priority=u % 2)
            return c
        lax.fori_loop(0, tm // SUBLANES, issue, 0)

    def drain_rows(s):
        def drain(g, c):
            for u in range(SUBLANES):
                row_copy(0, s, g, u).wait()
            return c
        lax.fori_loop(0, tm // SUBLANES, drain, 0)

    def block_start(s):
        ahead = idx_copy(nxt, 1 - s)
        ahead.start()
        drain_rows(s)
        xbf[...] = xbuf[s].reshape(xbf.shape).astype(BF16)
        ahead.wait()
        issue_rows(1 - s)

    @pl.when(jnp.logical_and(valid, j == 0))
    def _():
        @pl.when(i == 0)
        def _():
            first = idx_copy(0, 0)
            first.start()
            first.wait()
            issue_rows(0)

        for s in range(2):
            pl.when(slot == s)(functools.partial(block_start, s))

    @pl.when(valid)
    def _():
        x = xbf[...]
        a = _dot(x, wg_ref[...])
        b = _dot(x, wu_ref[...])
        m = (a * _sigmoid(a) * b).astype(BF16)
        contrib = _dot(m, wd_ref[...])

        @pl.when(j == 0)
        def _():
            o_ref[...] = contrib

        @pl.when(j > 0)
        def _():
            o_ref[...] += contrib

    is_last = jnp.logical_or(i == nb - 1, bv_ref[nxt] == 0)

    @pl.when(jnp.logical_and(jnp.logical_and(valid, is_last), j == pl.num_programs(1) - 1))
    def _():
        for s in range(2):
            pl.when(slot == s)(functools.partial(drain_rows, 1 - s))

    @pl.when(jnp.logical_and(jnp.logical_not(valid), j == 0))
    def _():
        o_ref[...] = jnp.zeros_like(o_ref)


def _moe(block_e, block_valid, tok_sorted, h, wg, wu, wd, tm):
    p = tok_sorted.shape[0]
    d = h.shape[1]
    ff = wg.shape[2]
    tf = _tile(ff, 1792, LANES)
    nf = ff // tf
    nb = p // tm

    def wcol(i, j, be, bv):
        return (be[i], 0, jnp.where(bv[i] != 0, j, nf - 1))

    def wrow(i, j, be, bv):
        return (be[i], jnp.where(bv[i] != 0, j, nf - 1), 0)

    est = 2 * (3 * d * tf * 2 + tm * d * 4) + 2 * tm * d * 4 + tm * d * 2 + 3 * tm * tf * 4
    grid_spec = pltpu.PrefetchScalarGridSpec(
        num_scalar_prefetch=2,
        grid=(nb, nf),
        in_specs=[pl.BlockSpec(memory_space=pl.ANY),
                  pl.BlockSpec(memory_space=pl.ANY),
                  pl.BlockSpec((None, d, tf), wcol),
                  pl.BlockSpec((None, d, tf), wcol),
                  pl.BlockSpec((None, tf, d), wrow)],
        out_specs=pl.BlockSpec((tm, d), lambda i, j, be, bv: (i, 0)),
        scratch_shapes=[pltpu.SMEM((2 * tm,), jnp.int32),
                        pltpu.VMEM((2, tm // SUBLANES, SUBLANES, d), F32),
                        pltpu.VMEM((tm, d), BF16), pltpu.SemaphoreType.DMA((2,)),
                        pltpu.SemaphoreType.DMA((2,))],
    )
    return pl.pallas_call(
        functools.partial(_moe_kernel, tm=tm),
        grid_spec=grid_spec,
        out_shape=jax.ShapeDtypeStruct((p, d), F32),
        compiler_params=_params(("arbitrary", "arbitrary"), est),
        name="moe",
    )(block_e, block_valid, tok_sorted, h, wg, wu, wd)


def _combine_kernel(pos_hbm, y_hbm, x_ref, route_ref, g_ref, o_ref, idx_smem, ybuf, sem_idx,
                    sem_rows, *, tm):
    i = pl.program_id(0)
    slot = i % 2
    nrows = 2 * tm

    def row_copy(src, s, g, u):
        return pltpu.make_async_copy(y_hbm.at[pl.ds(src, 1)], ybuf.at[s, g, pl.ds(u, 1)],
                                     sem_rows.at[s])

    def fetch(tile, s):
        idx_copy = pltpu.make_async_copy(pos_hbm.at[pl.ds(tile * nrows, nrows)],
                                         idx_smem.at[pl.ds(s * nrows, nrows)], sem_idx.at[s])
        idx_copy.start()
        idx_copy.wait()

        def issue(g, c):
            for u in range(SUBLANES):
                row_copy(idx_smem[s * nrows + g * SUBLANES + u], s, g, u).start(priority=u % 2)
            return c

        lax.fori_loop(0, nrows // SUBLANES, issue, 0)

    @pl.when(i == 0)
    def _():
        fetch(0, 0)

    def tile_body(s):
        @pl.when(i + 1 < pl.num_programs(0))
        def _():
            fetch(i + 1, 1 - s)

        def drain(g, c):
            for u in range(SUBLANES):
                row_copy(0, s, g, u).wait()
            return c

        lax.fori_loop(0, nrows // SUBLANES, drain, 0)
        route = route_ref[...]
        g1 = route[:, TOP_K:TOP_K + 1]
        g2 = route[:, TOP_K + 1:TOP_K + 2]
        nt = tm // SUBLANES
        y1 = ybuf[s, 0:nt].reshape(x_ref.shape)
        y2 = ybuf[s, nt:2 * nt].reshape(x_ref.shape)
        x = x_ref[...] + g1 * y1 + g2 * y2
        o_ref[...] = _rms(x, g_ref[...])

    for s in range(2):
        pl.when(slot == s)(functools.partial(tile_body, s))


def _combine(pos, y_sorted, x, route, g, tm):
    t, d = x.shape
    est = 2 * (2 * tm * d * 4 + tm * LANES * 4) + 4 * tm * d * 4 + 2 * tm * d * 4
    return pl.pallas_call(
        functools.partial(_combine_kernel, tm=tm),
        grid=(t // tm,),
        in_specs=[pl.BlockSpec(memory_space=pl.ANY),
                  pl.BlockSpec(memory_space=pl.ANY),
                  pl.BlockSpec((tm, d), lambda i: (i, 0)),
                  pl.BlockSpec((tm, LANES), lambda i: (i, 0)),
                  pl.BlockSpec((1, d), lambda i: (0, 0))],
        out_specs=pl.BlockSpec((tm, d), lambda i: (i, 0)),
        out_shape=jax.ShapeDtypeStruct((t, d), F32),
        scratch_shapes=[pltpu.SMEM((4 * tm,), jnp.int32),
                        pltpu.VMEM((2, 2 * tm // SUBLANES, SUBLANES, d), F32),
                        pltpu.SemaphoreType.DMA((2,)), pltpu.SemaphoreType.DMA((2,))],
        compiler_params=_params(("arbitrary",), est),
        name="combine",
    )(pos, y_sorted, x, route, g.reshape(1, d))


def _route_plan(route, tm):
    t = route.shape[0]
    flat_e = route[:, 0:TOP_K].astype(jnp.int32).reshape(-1)
    flat_tok = jnp.repeat(jnp.arange(t, dtype=jnp.int32), TOP_K)
    onehot = (flat_e[:, None] == jnp.arange(N_EXPERTS, dtype=jnp.int32)[None, :]).astype(jnp.int32)
    csum = jnp.cumsum(onehot, axis=0)
    counts = csum[-1]
    rank = jnp.sum((csum - onehot) * onehot, axis=1)
    padded = ((counts + tm - 1) // tm) * tm
    pend = jnp.cumsum(padded)
    pstart = pend - padded
    dest = pstart[flat_e] + rank
    nb = (t * TOP_K + tm - 1) // tm + N_EXPERTS
    p = nb * tm
    tok_sorted = jnp.zeros((p,), jnp.int32).at[dest].set(flat_tok, unique_indices=True)
    block_start = jnp.arange(nb, dtype=jnp.int32) * tm
    block_valid = (block_start < pend[-1]).astype(jnp.int32)
    last_e = jnp.sum((pend <= pend[-1] - 1).astype(jnp.int32))
    block_e = jnp.sum((pend[None, :] <= block_start[:, None]).astype(jnp.int32), axis=1)
    block_e = jnp.where(block_valid != 0, jnp.minimum(block_e, N_EXPERTS - 1), last_e)
    return tok_sorted, block_e.astype(jnp.int32), block_valid, dest.reshape(t, TOP_K)


def _tile_pos(dest, tm):
    t = dest.shape[0]
    return jnp.transpose(dest.reshape(t // tm, tm, TOP_K), (0, 2, 1)).reshape(-1)


def _rwkv_perm():
    j = np.arange(A_WIDTH)
    return (j % A_HEADS) * A_HEAD_DIM + j // A_HEADS


def _rwkv_cols():
    perm = _rwkv_perm()
    return np.concatenate([perm, A_WIDTH + perm, 2 * A_WIDTH + perm,
                           np.arange(3 * A_WIDTH, A_COLS)])


def _prep_weights(w):
    bf = lambda a: a.astype(BF16)
    perm = _rwkv_perm()
    cols = _rwkv_cols()
    head = np.arange(A_WIDTH) % A_HEADS
    lora = jnp.zeros((A_LORA_COLS, A_WIDTH), F32)
    out = dict(w)
    out['ones'] = jnp.asarray(head[:, None] == head[None, :], BF16)
    out['wdec_pad'] = bf(lora.at[0:64].set(w['a_w_decay'][0])[:, perm])
    out['wiclr_pad'] = bf(lora.at[64:128].set(w['a_w_iclr'][0])[:, perm])
    out['wgate_pad'] = bf(lora.at[128:256].set(w['a_w_gate'][0])[:, perm])
    out['mu_p'] = w['a_mu'][0][cols].reshape(1, -1)
    for name in ('a_w0', 'a_a0', 'a_k_a', 'a_r_k', 'a_lnx_g', 'a_lnx_b'):
        out[name + '_p'] = w[name][0].reshape(-1)[perm]
    for name in ('ffn_gate', 'ffn_up', 'ffn_down', 'w_in1', 'w_out1', 'c_w_a',
                 'c_w_x', 'moe_gate', 'moe_up', 'moe_down'):
        out[name] = bf(w[name][0])
    in_cols = np.concatenate([cols, np.arange(A_COLS, A_COLS + B_COLS)])
    out['w_in0'] = bf(w['w_in0'][0][:, in_cols])
    out_rows = np.concatenate([perm, np.arange(A_WIDTH, A_WIDTH + B_V_WIDTH)])
    out['w_out0'] = bf(w['w_out0'][0][out_rows])
    out['router_pad'] = jnp.zeros((w['moe_router'].shape[1], LANES), F32).at[:, :N_EXPERTS].set(
        w['moe_router'][0])
    return out


def _layer0(x, bn, length, pos0, shift, wkv, ret, w):
    t = bn * length
    cols = _rwkv_cols()
    pa, pb = _norm_matmul(x, w['norm_mix0'][0], w['w_in0'], (A_COLS, B_COLS))
    tm = _tile(t, 256) if length >= 256 else _tile(t, 256, length)
    shift = shift.astype(F32)[:, cols]
    if length >= tm:
        starts = jnp.arange(t // tm) * tm
        before = pa[jnp.maximum(starts - 1, 0)]
        first = jnp.where((starts % length == 0)[:, None], shift[starts // length], before)
        ext = jnp.zeros((t // tm, SUBLANES, A_COLS), F32).at[:, 0].set(first).reshape(-1, A_COLS)
    else:
        ext = jnp.repeat(shift, length, axis=0)
    row = lambda a: a.reshape(1, -1)
    cm = bn * A_HEADS <= LANES and length % LANES == 0 and tm % LANES == 0 and length >= tm
    outs = _rwkv_prep(
        pa, ext, length, tm,
        (w['mu_p'], row(w['a_w0_p']), w['wdec_pad'], row(w['a_a0_p']), w['wiclr_pad'],
         w['wgate_pad']), cm)
    r, dec, k, a, v, g = outs[:6]
    ops = outs[6:] if cm else (r, dec, k, a, v)
    y, wkv_new = _rwkv_recurrence(ops, w['a_k_k'][0], w['a_k_a'][0], wkv, bn, length, cm)
    yb, ret_new = _retention(pb, ret, bn, length, pos0)
    x = _mix_out(x, y, r, k, a, v, g, yb, w['a_k_a_p'], w['a_r_k_p'], w['a_lnx_g_p'],
                 w['a_lnx_b_p'], w['ones'], w['w_out0'], length, cm)
    x = _ffn(x, w['norm_ffn0'][0], w['ffn_gate'], w['ffn_up'], w['ffn_down'])
    shift_new = pa.reshape(bn, length, A_COLS)[:, -1][:, np.argsort(cols)]
    return x, shift_new, wkv_new, ret_new


def _layer1_mixer(x, bn, length, conv, hlru, w):
    assert length >= CONV_W - 1
    gate, xb = _norm_matmul(x, w['norm_mix1'][0], w['w_in1'], (D_RNN, D_RNN))
    tail0 = jnp.zeros((bn, SUBLANES, D_RNN), F32).at[:, SUBLANES - (CONV_W - 1):].set(conv.astype(F32))
    row = lambda a: a[0].reshape(1, -1)
    y, h_last = _rglru(gate, xb, tail0, hlru.astype(F32).reshape(bn, 1, D_RNN),
                       (w['c_conv_w'][0], row(w['c_conv_b']), w['c_w_a'], row(w['c_b_a']),
                        w['c_w_x'], row(w['c_b_x']), row(w['c_lambda'])), bn, length)
    x = _matmul_res(y, w['w_out1'], x)
    conv_new = xb.reshape(bn, length, D_RNN)[:, length - (CONV_W - 1):]
    return x, conv_new, h_last.reshape(bn, D_RNN)


def kernel(x_prompt, x_sample, state_rwkv_shift, state_rwkv_wkv, state_ret, state_lru_conv, state_lru_h, norm_mix0, w_in0, a_mu, a_w0, a_w_decay, a_a0, a_w_iclr, a_w_gate, a_k_k, a_k_a, a_r_k, a_lnx_g, a_lnx_b, w_out0, norm_ffn0, ffn_gate, ffn_up, ffn_down, norm_mix1, w_in1, c_conv_w, c_conv_b, c_w_a, c_b_a, c_w_x, c_b_x, c_lambda, w_out1, norm_ffn1, moe_router, moe_gate, moe_up, moe_down, norm_final):
    w = _prep_weights(dict(
        norm_mix0=norm_mix0, w_in0=w_in0, a_mu=a_mu, a_w0=a_w0, a_w_decay=a_w_decay, a_a0=a_a0,
        a_w_iclr=a_w_iclr, a_w_gate=a_w_gate, a_k_k=a_k_k, a_k_a=a_k_a, a_r_k=a_r_k, a_lnx_g=a_lnx_g,
        a_lnx_b=a_lnx_b, w_out0=w_out0, norm_ffn0=norm_ffn0, ffn_gate=ffn_gate, ffn_up=ffn_up,
        ffn_down=ffn_down, norm_mix1=norm_mix1, w_in1=w_in1, c_conv_w=c_conv_w, c_conv_b=c_conv_b,
        c_w_a=c_w_a, c_b_a=c_b_a, c_w_x=c_w_x, c_b_x=c_b_x, c_lambda=c_lambda, w_out1=w_out1,
        norm_ffn1=norm_ffn1, moe_router=moe_router, moe_gate=moe_gate, moe_up=moe_up,
        moe_down=moe_down))
    dt = x_prompt.dtype
    d = x_prompt.shape[-1]
    bp, lp = x_prompt.shape[:2]
    bs, ls = x_sample.shape[:2]
    zeros = lambda shape: jnp.zeros(shape, F32)
    groups = [
        (x_prompt.reshape(bp * lp, d), bp, lp, 0, zeros((bp, A_COLS)),
         zeros((bp, A_HEADS, A_HEAD_DIM, A_HEAD_DIM)), zeros((bp, B_HEADS, B_QK_DIM, B_V_DIM)),
         zeros((bp, CONV_W - 1, D_RNN)), zeros((bp, D_RNN))),
        (x_sample.reshape(bs * ls, d), bs, ls, PAST_LEN, state_rwkv_shift[0], state_rwkv_wkv[0],
         state_ret[0], state_lru_conv[0], state_lru_h[0]),
    ]
    xs, states, hs, routes = [], [], [], []
    for x, bn, length, pos0, shift, wkv, ret, conv, hlru in groups:
        x, s_shift, s_wkv, s_ret = _layer0(x, bn, length, pos0, shift, wkv, ret, w)
        x, s_conv, s_h = _layer1_mixer(x, bn, length, conv, hlru, w)
        h, route = _router(x, w['norm_ffn1'][0], w['router_pad'])
        xs.append(x)
        hs.append(h)
        routes.append(route)
        states.append((s_shift, s_wkv, s_ret, s_conv, s_h))
    h_all = jnp.concatenate(hs, 0)
    route_all = jnp.concatenate(routes, 0)
    tm_moe = 512 if h_all.shape[0] >= 4096 else 64
    tok_sorted, block_e, block_valid, dest = _route_plan(route_all, tm_moe)
    y_sorted = _moe(block_e, block_valid, tok_sorted, h_all, w['moe_gate'], w['moe_up'],
                    w['moe_down'], tm_moe)
    outs = []
    off = 0
    for x, route in zip(xs, routes):
        t = x.shape[0]
        tm = _tile(t, 256)
        pos = _tile_pos(dest[off:off + t], tm)
        outs.append(_combine(pos, y_sorted, x, route, norm_final, tm))
        off += t
    y_prompt = outs[0].reshape(bp, lp, d)
    y_sample = outs[1].reshape(bs, ls, d)
    st_p = tuple(s[None].astype(dt) for s in states[0])
    st_s = tuple(s[None].astype(dt) for s in states[1])
    return (y_prompt, y_sample) + st_p + st_s
```
